```python
import numpy as np
import jax
import jax.numpy as jnp
from jax import lax

D_MODEL = 2048
BATCH = 1
SEQ = 16384
DEPTH = 2

HEAD_DIM = 64
N_NSA_HEADS = D_MODEL // (2 * HEAD_DIM)
N_NSA_KV = N_NSA_HEADS // 4
NSA_HPG = N_NSA_HEADS // N_NSA_KV
N_SB_HEADS = D_MODEL // (4 * HEAD_DIM)
N_FOX_HEADS = D_MODEL // (4 * HEAD_DIM)
MIX_WIDTH = (N_NSA_HEADS + N_SB_HEADS + N_FOX_HEADS) * HEAD_DIM
CMP_LEN = 32
CMP_STRIDE = 16
SLC_LEN = 64
SLC_TOPK = 8
WINDOW = 512
Q_BLOCK = 128
N_GROUPS = 4
EXPERTS_PER_GROUP = 4
N_EXPERTS = N_GROUPS * EXPERTS_PER_GROUP
INNER_TOPK = 2
D_FF_EXPERT = D_MODEL // 8
ALPHA = (2 * DEPTH) ** 0.25
BETA_INIT = (8 * DEPTH) ** -0.25
EPS = 1e-5
NEG_BIG = -1e30
SEL_BONUS = 1e6

NSA_Q_W = N_NSA_HEADS * HEAD_DIM
NSA_KV_W = N_NSA_KV * HEAD_DIM
SB_W = N_SB_HEADS * HEAD_DIM
FOX_W = N_FOX_HEADS * HEAD_DIM
PROJ_SIZES = (NSA_Q_W, NSA_KV_W, NSA_KV_W, NSA_KV_W, NSA_KV_W, NSA_KV_W, NSA_KV_W, 3 * N_NSA_HEADS, SB_W, SB_W, SB_W, FOX_W, FOX_W, FOX_W, N_FOX_HEADS)
V_PARTS = (2, 4, 6, 10, 13)
PROJ_DIM = sum(PROJ_SIZES)

kernel_name = 'hybrid_nsa_stickbreak_fox_hmoe'

F32 = jnp.float32


def layer_norm(x, g, b):
    xf = x.astype(F32)
    mu = jnp.mean(xf, axis=-1, keepdims=True)
    var = jnp.mean(jnp.square(xf - mu), axis=-1, keepdims=True)
    return ((xf - mu) * lax.rsqrt(var + EPS) * g + b).astype(x.dtype)


def rms_norm(x, g):
    xf = x.astype(F32)
    return (xf * lax.rsqrt(jnp.mean(jnp.square(xf), axis=-1, keepdims=True) + EPS) * g).astype(x.dtype)


def alibi_slopes(n):
    return 2.0 ** (-8.0 * jnp.arange(1, n + 1, dtype=F32) / n)


def masked_softmax(s, mask):
    return jax.nn.softmax(jnp.where(mask, s, NEG_BIG), axis=-1)


def strict_lower(n):
    r = jnp.arange(n)
    return (r[:, None] > r[None, :]).astype(F32)


def cmp_to_slc_matrix(n_cmp, n_slc):
    r, c = SLC_LEN // CMP_STRIDE, CMP_LEN // CMP_STRIDE
    offs = (np.arange(r)[:, None] + np.arange(c)[None, :]).reshape(-1)
    j = np.arange(n_slc)
    idx = r * j[None, :] + offs[:, None]
    jj = np.broadcast_to(j[None, :], idx.shape)
    ok = idx < n_cmp
    m = np.zeros((n_cmp, n_slc), np.float32)
    np.add.at(m, (idx[ok], jj[ok]), 1.0)
    return m


def compress_blocks(kv, pos, w1, w2):
    s = kv.shape[1]
    n_cmp = (s - CMP_LEN) // CMP_STRIDE + 1
    idx = np.arange(n_cmp)[:, None] * CMP_STRIDE + np.arange(CMP_LEN)[None, :]
    blk = kv[:, idx] + pos[None, None, :, None, :]
    h = jax.nn.gelu(jnp.einsum('bnlgd,lde->bgne', blk, w1))
    return jnp.einsum('bgne,ef->bgnf', h, w2)


def nsa_mixer(q, k_cmp, v_cmp, k_slc, v_slc, k_win, v_win, gates):
    b, g, hg, s, dh = q.shape
    n_cmp = k_cmp.shape[2]
    n_slc = s // SLC_LEN
    k_sel = min(SLC_TOPK, n_slc)
    scale = dh ** -0.5
    slopes = alibi_slopes(g * hg).reshape(1, g, hg, 1, 1)
    cmp_end = jnp.arange(n_cmp) * CMP_STRIDE + (CMP_LEN - 1)
    imp_map = jnp.asarray(cmp_to_slc_matrix(n_cmp, n_slc))
    blk_start = jnp.arange(n_slc) * SLC_LEN
    jj = jnp.arange(n_slc)
    kb = k_slc.reshape(b, g, n_slc, SLC_LEN, dh)
    vb = v_slc.reshape(b, g, n_slc, SLC_LEN, dh)
    pad = ((0, 0), (0, 0), (WINDOW, 0), (0, 0))
    kw_pad = jnp.pad(k_win, pad)
    vw_pad = jnp.pad(v_win, pad)
    b_ix = jnp.arange(b)[:, None, None, None]
    g_ix = jnp.arange(g)[None, :, None, None]

    def block(qb):
        t0 = qb * Q_BLOCK
        t = t0 + jnp.arange(Q_BLOCK)
        qt = lax.dynamic_slice_in_dim(q, t0, Q_BLOCK, axis=3)
        gt = jax.nn.sigmoid(lax.dynamic_slice_in_dim(gates, t0, Q_BLOCK, axis=3).astype(F32))
        dist_c = (t[:, None] - cmp_end[None, :]).astype(F32)
        mask_c = dist_c >= 0
        s_c = jnp.einsum('bghqd,bgnd->bghqn', qt, k_cmp).astype(F32) * scale
        p_c = masked_softmax(s_c - slopes * dist_c, mask_c) * mask_c
        o_c = jnp.einsum('bghqn,bgnd->bghqd', p_c, v_cmp)
        imp = jnp.einsum('bgqn,nj->bgqj', jnp.sum(p_c, axis=2), imp_map)
        cur = t // SLC_LEN
        forced = (jj[None, :] == 0) | (jj[None, :] == cur[:, None]) | (jj[None, :] == cur[:, None] - 1)
        valid = blk_start[None, :] <= t[:, None]
        score = jnp.where(valid, imp + SEL_BONUS * forced, -jnp.inf)
        _, sel = lax.top_k(score, k_sel)
        ks = kb[b_ix, g_ix, sel].reshape(b, g, Q_BLOCK, k_sel * SLC_LEN, dh)
        vs = vb[b_ix, g_ix, sel].reshape(b, g, Q_BLOCK, k_sel * SLC_LEN, dh)
        tok = (sel[..., None] * SLC_LEN + jnp.arange(SLC_LEN)).reshape(b, g, Q_BLOCK, k_sel * SLC_LEN)
        dist_s = (t[None, None, :, None] - tok).astype(F32)[:, :, None]
        s_s = jnp.einsum('bghqd,bgqkd->bghqk', qt, ks).astype(F32) * scale
        p_s = masked_softmax(s_s - slopes * dist_s, dist_s >= 0)
        o_s = jnp.einsum('bghqk,bgqkd->bghqd', p_s, vs)
        kw = lax.dynamic_slice_in_dim(kw_pad, t0, WINDOW + Q_BLOCK, axis=2)
        vw = lax.dynamic_slice_in_dim(vw_pad, t0, WINDOW + Q_BLOCK, axis=2)
        kpos = t0 - WINDOW + jnp.arange(WINDOW + Q_BLOCK)
        dist_w = (t[:, None] - kpos[None, :]).astype(F32)
        mask_w = (dist_w >= 0) & (dist_w < WINDOW) & (kpos[None, :] >= 0)
        s_w = jnp.einsum('bghqd,bgkd->bghqk', qt, kw).astype(F32) * scale
        p_w = masked_softmax(s_w - slopes * dist_w, mask_w)
        o_w = jnp.einsum('bghqk,bgkd->bghqd', p_w, vw)
        o = gt[..., 0:1] * o_c + gt[..., 1:2] * o_s + gt[..., 2:3] * o_w
        return o.astype(q.dtype)

    out = lax.map(block, jnp.arange(s // Q_BLOCK))
    return out.transpose(1, 0, 4, 2, 3, 5).reshape(b, s, g * hg * dh)


def stick_breaking_mixer(q, k, v):
    b, h, s, dh = q.shape
    q = q * (dh ** -0.5)
    tri = strict_lower(Q_BLOCK)
    outs = []
    for i in range(s // Q_BLOCK):
        t0 = i * Q_BLOCK
        n_kb = i + 1
        kl = t0 + Q_BLOCK
        t = t0 + jnp.arange(Q_BLOCK)
        z = jnp.einsum('bhqd,bhkd->bhqk', q[:, :, t0:kl], k[:, :, :kl]).astype(F32)
        mask = jnp.arange(kl)[None, :] < t[:, None]
        sp = jax.nn.softplus(z)
        log_1mb = jnp.where(mask, -sp, 0.0).reshape(b, h, Q_BLOCK, n_kb, Q_BLOCK)
        within = jnp.einsum('bhqnj,jk->bhqnk', log_1mb, tri)
        across = jnp.einsum('bhqm,mn->bhqn', jnp.sum(log_1mb, axis=-1), strict_lower(n_kb))
        suffix = (within + across[..., None]).reshape(b, h, Q_BLOCK, kl)
        a = jnp.where(mask, jnp.exp(z - sp + suffix), 0.0)
        outs.append(jnp.einsum('bhqk,bhkd->bhqd', a, v[:, :, :kl]).astype(v.dtype))
    out = jnp.concatenate(outs, axis=2)
    return out.transpose(0, 2, 1, 3).reshape(b, s, h * dh)


def forgetting_mixer(q, k, v, log_f):
    b, h, s, dh = q.shape
    q = q * (dh ** -0.5)
    c = jnp.cumsum(log_f, axis=-1)
    outs = []
    for i in range(s // Q_BLOCK):
        t0 = i * Q_BLOCK
        kl = t0 + Q_BLOCK
        t = t0 + jnp.arange(Q_BLOCK)
        logits = jnp.einsum('bhqd,bhkd->bhqk', q[:, :, t0:kl], k[:, :, :kl]).astype(F32) + c[:, :, t0:kl, None] - c[:, :, None, :kl]
        p = masked_softmax(logits, jnp.arange(kl)[None, :] <= t[:, None])
        outs.append(jnp.einsum('bhqk,bhkd->bhqd', p, v[:, :, :kl]).astype(v.dtype))
    out = jnp.concatenate(outs, axis=2)
    return out.transpose(0, 2, 1, 3).reshape(b, s, h * dh)


def hybrid_mixer(x, w_in, cmp_pos_k, cmp_w1_k, cmp_w2_k, cmp_pos_v, cmp_w1_v, cmp_w2_v, fox_forget_bias, norm_nsa, norm_sb, norm_fox, w_out):
    b, s, _ = x.shape
    g, hg = N_NSA_KV, NSA_HPG
    parts = jnp.split(x @ w_in, np.cumsum(PROJ_SIZES)[:-1].tolist(), axis=-1)
    (nq, ck, cv, sk, sv, wk, wv, ng, sbq, sbk, sbv, fq, fk, fv, ff) = parts
    group_kv = lambda t: t.reshape(b, s, g, HEAD_DIM)
    to_bgsd = lambda t: t.reshape(b, s, g, HEAD_DIM).transpose(0, 2, 1, 3)
    heads = lambda t, n: t.reshape(b, s, n, HEAD_DIM).transpose(0, 2, 1, 3)
    q_nsa = nq.reshape(b, s, g, hg, HEAD_DIM).transpose(0, 2, 3, 1, 4)
    k_cmp = compress_blocks(group_kv(ck), cmp_pos_k, cmp_w1_k, cmp_w2_k)
    v_cmp = compress_blocks(group_kv(cv), cmp_pos_v, cmp_w1_v, cmp_w2_v)
    gates = ng.reshape(b, s, g, hg, 3).transpose(0, 2, 3, 1, 4)
    o_nsa = nsa_mixer(q_nsa, k_cmp, v_cmp, to_bgsd(sk), to_bgsd(sv), to_bgsd(wk), to_bgsd(wv), gates)
    o_sb = stick_breaking_mixer(heads(sbq, N_SB_HEADS), heads(sbk, N_SB_HEADS), heads(sbv, N_SB_HEADS))
    log_f = jax.nn.log_sigmoid((ff + fox_forget_bias).astype(F32)).transpose(0, 2, 1)
    o_fox = forgetting_mixer(heads(fq, N_FOX_HEADS), heads(fk, N_FOX_HEADS), heads(fv, N_FOX_HEADS), log_f)
    y = jnp.concatenate([rms_norm(o_nsa, norm_nsa), rms_norm(o_sb, norm_sb), rms_norm(o_fox, norm_fox)], axis=-1)
    return y @ w_out


def hierarchical_moe(x, rg_w, rg_b, re_w, re_b, w_gate, w_up, w_down):
    b, s, d = x.shape
    xt = x.reshape(b * s, d)
    n_tok = b * s
    p_grp = jax.nn.softmax((xt @ rg_w + rg_b).astype(F32), axis=-1)
    g_sel = jnp.argmax(p_grp, axis=-1)
    g_w = jnp.max(p_grp, axis=-1)
    e_logits = (xt @ re_w + re_b).astype(F32).reshape(n_tok, N_GROUPS, EXPERTS_PER_GROUP)
    e_logits = e_logits[jnp.arange(n_tok), g_sel]
    top_v, top_i = lax.top_k(e_logits, INNER_TOPK)
    w = jax.nn.softmax(top_v, axis=-1) * g_w[:, None]
    expert_id = g_sel[:, None] * EXPERTS_PER_GROUP + top_i
    gate = jnp.sum(jax.nn.one_hot(expert_id, N_EXPERTS, dtype=F32) * w[..., None], axis=1)

    def expert_step(acc, inp):
        wg, wu, wd, gcol = inp
        hid = jax.nn.silu(xt @ wg) * (xt @ wu)
        return acc + (hid * gcol[:, None]) @ wd, None

    y, _ = lax.scan(expert_step, jnp.zeros_like(xt), (w_gate, w_up, w_down, gate.T.astype(x.dtype)))
    return y.reshape(b, s, d)


def setup_inputs(seed: int = 0) -> dict:
    key = jax.random.key(seed)
    ks = jax.random.split(key, 26)
    L, D, dh = DEPTH, D_MODEL, HEAD_DIM
    nrm = lambda k, shape, sc: jax.random.normal(k, shape, F32) * sc
    col_scale = np.concatenate([np.full(n, BETA_INIT if i in V_PARTS else 1.0, np.float32) for i, n in enumerate(PROJ_SIZES)])
    return {
        'x': nrm(ks[0], (BATCH, SEQ, D), 1.0),
        'w_in': nrm(ks[1], (L, D, PROJ_DIM), D ** -0.5) * jnp.asarray(col_scale),
        'cmp_pos_k': nrm(ks[2], (L, CMP_LEN, dh), 0.1),
        'cmp_w1_k': nrm(ks[3], (L, CMP_LEN, dh, dh), (CMP_LEN * dh) ** -0.5),
        'cmp_w2_k': nrm(ks[4], (L, dh, dh), dh ** -0.5),
        'cmp_pos_v': nrm(ks[5], (L, CMP_LEN, dh), 0.1),
        'cmp_w1_v': nrm(ks[6], (L, CMP_LEN, dh, dh), (CMP_LEN * dh) ** -0.5),
        'cmp_w2_v': nrm(ks[7], (L, dh, dh), dh ** -0.5),
        'fox_forget_bias': jnp.linspace(1.0, 6.0, N_FOX_HEADS, dtype=F32)[None, :] + nrm(ks[8], (L, N_FOX_HEADS), 0.1),
        'norm_nsa': 1.0 + nrm(ks[9], (L, NSA_Q_W), 0.01),
        'norm_sb': 1.0 + nrm(ks[10], (L, SB_W), 0.01),
        'norm_fox': 1.0 + nrm(ks[11], (L, FOX_W), 0.01),
        'w_out': nrm(ks[12], (L, MIX_WIDTH, D), MIX_WIDTH ** -0.5 * BETA_INIT),
        'ln1_g': 1.0 + nrm(ks[13], (L, D), 0.01),
        'ln1_b': nrm(ks[14], (L, D), 0.01),
        'router_group_w': nrm(ks[15], (L, D, N_GROUPS), D ** -0.5),
        'router_group_b': nrm(ks[16], (L, N_GROUPS), 0.01),
        'router_expert_w': nrm(ks[17], (L, D, N_EXPERTS), D ** -0.5),
        'router_expert_b': nrm(ks[18], (L, N_EXPERTS), 0.01),
        'expert_w_gate': nrm(ks[19], (L, N_EXPERTS, D, D_FF_EXPERT), D ** -0.5 * BETA_INIT),
        'expert_w_up': nrm(ks[20], (L, N_EXPERTS, D, D_FF_EXPERT), D ** -0.5 * BETA_INIT),
        'expert_w_down': nrm(ks[21], (L, N_EXPERTS, D_FF_EXPERT, D), D_FF_EXPERT ** -0.5 * BETA_INIT),
        'ln2_g': 1.0 + nrm(ks[22], (L, D), 0.01),
        'ln2_b': nrm(ks[23], (L, D), 0.01),
    }


def reference(x, w_in, cmp_pos_k, cmp_w1_k, cmp_w2_k, cmp_pos_v, cmp_w1_v, cmp_w2_v, fox_forget_bias, norm_nsa, norm_sb, norm_fox, w_out, ln1_g, ln1_b, router_group_w, router_group_b, router_expert_w, router_expert_b, expert_w_gate, expert_w_up, expert_w_down, ln2_g, ln2_b):
    h = x
    for l in range(DEPTH):
        mix = hybrid_mixer(h, w_in[l], cmp_pos_k[l], cmp_w1_k[l], cmp_w2_k[l], cmp_pos_v[l], cmp_w1_v[l], cmp_w2_v[l], fox_forget_bias[l], norm_nsa[l], norm_sb[l], norm_fox[l], w_out[l])
        h = layer_norm(ALPHA * h + mix, ln1_g[l], ln1_b[l])
        ffn = hierarchical_moe(h, router_group_w[l], router_group_b[l], router_expert_w[l], router_expert_b[l], expert_w_gate[l], expert_w_up[l], expert_w_down[l])
        h = layer_norm(ALPHA * h + ffn, ln2_g[l], ln2_b[l])
    return h
```

```python
import functools

import numpy as np
import jax
import jax.numpy as jnp
from jax import lax
from jax.experimental import pallas as pl
from jax.experimental.pallas import tpu as pltpu

F32 = jnp.float32
BF16 = jnp.bfloat16

D_MODEL = 2048
DEPTH = 2
HEAD_DIM = 64
N_NSA_HEADS = 16
N_NSA_KV = 4
NSA_HPG = 4
N_SB_HEADS = 8
N_FOX_HEADS = 8
CMP_LEN = 32
CMP_STRIDE = 16
SLC_LEN = 64
SLC_TOPK = 8
WINDOW = 512
N_GROUPS = 4
EXPERTS_PER_GROUP = 4
N_EXPERTS = 16
D_FF = D_MODEL // 8
ALPHA = (2 * DEPTH) ** 0.25
EPS = 1e-5
NEG_BIG = -1e30
SEL_BONUS = 1e6
QK_SCALE = HEAD_DIM ** -0.5

LANES = 128
NSA_QB = 128
SLC_CHUNK = 256
BLK_PER_CHUNK = SLC_CHUNK // SLC_LEN
SB_BLK = 128
FOX_BLK = 256
EXP_UNDERFLOW = -104.0
MAIN_W = 5632
VMEM_LIMIT = 56 * 1024 * 1024

_PROJ_SIZES = (1024, 256, 256, 256, 256, 256, 256, 48, 512, 512, 512, 512, 512, 512, 8)
_OFF = np.concatenate([[0], np.cumsum(_PROJ_SIZES)])


def _perm_columns():
    seg = lambda i: np.arange(_OFF[i], _OFF[i + 1])
    nq, ck, cv, sk, sv, wk, wv, ng, sbq, sbk, sbv, fq, fk, fv, ff = [seg(i) for i in range(15)]

    def interleave(k, v, n):
        return np.concatenate([np.concatenate([k[h * 64:(h + 1) * 64], v[h * 64:(h + 1) * 64]]) for h in range(n)])

    main = np.concatenate([nq, ck, cv, interleave(sk, sv, 4), interleave(wk, wv, 4), sbq,
                           interleave(sbk, sbv, 8), fq, interleave(fk, fv, 8)])
    tail = np.concatenate([ng, ff])
    assert main.size == MAIN_W
    return main, tail


_PERM_MAIN, _PERM_TAIL = _perm_columns()


def _cparams(sem):
    return pltpu.CompilerParams(dimension_semantics=sem, vmem_limit_bytes=VMEM_LIMIT)


def _dot(a, b):
    return jnp.dot(a, b, preferred_element_type=F32)


def _dot_nt(a, b):
    return lax.dot_general(a, b, (((1,), (1,)), ((), ())), preferred_element_type=F32)


def _split3(x):
    hi = x.astype(BF16)
    r = x - hi.astype(F32)
    mid = r.astype(BF16)
    lo = (r - mid.astype(F32)).astype(BF16)
    return hi, mid, lo


def _dot_split(x, w, parts):
    pieces = _split3(x)[:parts]
    out = _dot(pieces[0], w)
    for p in pieces[1:]:
        out = out + _dot(p, w)
    return out


def _softplus(z):
    return jnp.maximum(z, 0.0) + jnp.log1p(jnp.exp(-jnp.abs(z)))


def _inproj_kernel(x_ref, w_ref, wt_ref, o_ref, ot_ref, xb_ref):
    @pl.when(pl.program_id(1) == 0)
    def _():
        xb = x_ref[...].astype(BF16)
        xb_ref[...] = xb
        ot_ref[...] = _dot(xb, wt_ref[...])

    o_ref[...] = _dot(xb_ref[...], w_ref[...]).astype(BF16)


def _inproj(h, w_main, w_tail):
    t, d = h.shape
    tm, tn = min(1024, t), 512
    return pl.pallas_call(
        _inproj_kernel,
        grid=(t // tm, MAIN_W // tn),
        in_specs=[pl.BlockSpec((tm, d), lambda i, j: (i, 0)),
                  pl.BlockSpec((d, tn), lambda i, j: (0, j)),
                  pl.BlockSpec((d, LANES), lambda i, j: (0, 0))],
        out_specs=[pl.BlockSpec((tm, tn), lambda i, j: (i, j)),
                   pl.BlockSpec((tm, LANES), lambda i, j: (i, 0))],
        out_shape=[jax.ShapeDtypeStruct((t, MAIN_W), BF16), jax.ShapeDtypeStruct((t, LANES), F32)],
        scratch_shapes=[pltpu.VMEM((tm, d), BF16)],
        compiler_params=_cparams(("parallel", "arbitrary")),
        name="inproj",
    )(h, w_main, w_tail)


def _cumsum_kernel(ff_ref, b_ref, c_ref, *, nb):
    x = ff_ref[...] + b_ref[...]
    lf = -_softplus(-x)
    r = lf.shape[0]
    jj = lax.broadcasted_iota(jnp.int32, (LANES, LANES), 0)
    kk = lax.broadcasted_iota(jnp.int32, (LANES, LANES), 1)
    upper = jnp.where(jj <= kk, 1.0, 0.0).astype(BF16)
    ones = jnp.ones((LANES, LANES), BF16)
    within = _dot_split(lf, upper, 3)
    tot = _dot_split(lf, ones, 3)
    rr = lax.broadcasted_iota(jnp.int32, (r, r), 0)
    cc = lax.broadcasted_iota(jnp.int32, (r, r), 1)
    earlier = jnp.where((cc < rr) & (cc // nb == rr // nb), 1.0, 0.0).astype(BF16)
    hi, mid, lo = _split3(tot)
    off = _dot(earlier, hi) + _dot(earlier, mid) + _dot(earlier, lo)
    c_ref[...] = within + off


def _forget_cumsum(ff_rows, bias_rows, nb):
    r = ff_rows.shape[0]
    return pl.pallas_call(
        functools.partial(_cumsum_kernel, nb=nb),
        out_shape=jax.ShapeDtypeStruct((r, LANES), F32),
        compiler_params=pltpu.CompilerParams(vmem_limit_bytes=VMEM_LIMIT),
        name="forget_cumsum",
    )(ff_rows, bias_rows)


def _gelu_tanh(x):
    return 0.5 * x * (1.0 + jnp.tanh(np.sqrt(2.0 / np.pi).astype(np.float32) * (x + 0.044715 * (x * x * x))))


def _compress_kernel(x_ref, pos_ref, wa_ref, wb_ref, w2_ref, o_ref):
    x = x_ref[...].astype(F32)
    n = x.shape[0]
    first = _dot((x + pos_ref[0:1, :]).astype(BF16), wa_ref[...])
    second = _dot((x + pos_ref[1:2, :]).astype(BF16), wb_ref[...])
    pre = first + pltpu.roll(second, n - 1, 0)
    o_ref[...] = _dot(_gelu_tanh(pre).astype(BF16), w2_ref[...]).astype(BF16)


def _compress(x2, pos2, wa, wb, w2):
    n = x2.shape[0]
    return pl.pallas_call(
        _compress_kernel,
        out_shape=jax.ShapeDtypeStruct((n, N_NSA_KV * HEAD_DIM), BF16),
        compiler_params=pltpu.CompilerParams(vmem_limit_bytes=VMEM_LIMIT),
        name="nsa_compress",
    )(x2, pos2, wa, wb, w2)


def _nsa_cmp_kernel(q_ref, kvc_ref, imap_ref, slope_ref, oc_ref, sel_ref, flag_ref):
    t0 = pl.program_id(1) * NSA_QB
    rows = NSA_HPG * NSA_QB
    q = q_ref[...].reshape(rows, LANES)
    kvc = kvc_ref[...]
    n = kvc.shape[0]
    n_slc = sel_ref.shape[-1]
    s = _dot_nt(q, kvc) * QK_SCALE
    t = t0 + lax.broadcasted_iota(jnp.int32, (rows, 1), 0) % NSA_QB
    cend = lax.broadcasted_iota(jnp.int32, (1, n), 1) * CMP_STRIDE + (CMP_LEN - 1)
    dist = (t - cend).astype(F32)
    mask = dist >= 0.0
    s = jnp.where(mask, s - slope_ref[...] * dist, NEG_BIG)
    m = jnp.max(s, axis=-1, keepdims=True)
    e = jnp.where(mask, jnp.exp(s - m), 0.0)
    l = jnp.sum(e, axis=-1, keepdims=True)
    p = e / jnp.where(l > 0.0, l, 1.0)
    oc_ref[...] = _dot(p.astype(BF16), kvc).reshape(NSA_HPG, NSA_QB, LANES)
    psum = p[0:NSA_QB] + p[NSA_QB:2 * NSA_QB] + p[2 * NSA_QB:3 * NSA_QB] + p[3 * NSA_QB:4 * NSA_QB]
    imp = _dot_split(psum, imap_ref[...], 2)

    tq = t0 + lax.broadcasted_iota(jnp.int32, (NSA_QB, 1), 0)
    jb = lax.broadcasted_iota(jnp.int32, (1, n_slc), 1)
    cur = tq // SLC_LEN
    forced = (jb == 0) | (jb == cur) | (jb == cur - 1)
    valid = jb * SLC_LEN <= tq
    score = jnp.where(valid, imp + jnp.where(forced, SEL_BONUS, 0.0), -jnp.inf)
    sel = jnp.zeros((NSA_QB, n_slc), F32)
    for _ in range(SLC_TOPK):
        best = jnp.max(score, axis=-1, keepdims=True)
        first = jnp.min(jnp.where(score == best, jb, n_slc), axis=-1, keepdims=True)
        hit = jb == first
        sel = jnp.where(hit, 1.0, sel)
        score = jnp.where(hit, -jnp.inf, score)
    sel = jnp.where(valid, sel, 0.0)
    sel_ref[...] = sel.astype(BF16)
    any_q = jnp.max(sel, axis=0, keepdims=True)
    jr = lax.broadcasted_iota(jnp.int32, (n_slc, LANES), 0)
    cr = lax.broadcasted_iota(jnp.int32, (n_slc, LANES), 1)
    group = jnp.where(jr // BLK_PER_CHUNK == cr, 1.0, 0.0).astype(BF16)
    per_chunk = _dot(jnp.broadcast_to(any_q, (8, n_slc)).astype(BF16), group)
    flag_ref[...] = jnp.where(per_chunk > 0.5, 1, 0).astype(jnp.int32)


def _nsa_cmp(q, kvc, imap, slopes):
    nh, t, _ = q.shape
    nq = t // NSA_QB
    n = kvc.shape[1]
    n_slc = t // SLC_LEN
    return pl.pallas_call(
        _nsa_cmp_kernel,
        grid=(N_NSA_KV, nq),
        in_specs=[pl.BlockSpec((NSA_HPG, NSA_QB, LANES), lambda g, i: (g, i, 0)),
                  pl.BlockSpec((None, n, LANES), lambda g, i: (g, 0, 0)),
                  pl.BlockSpec((n, n_slc), lambda g, i: (0, 0)),
                  pl.BlockSpec((None, NSA_HPG * NSA_QB, 1), lambda g, i: (g, 0, 0))],
        out_specs=[pl.BlockSpec((NSA_HPG, NSA_QB, LANES), lambda g, i: (g, i, 0)),
                   pl.BlockSpec((None, NSA_QB, n_slc), lambda g, i: (g, i, 0)),
                   pl.BlockSpec((None, None, 8, LANES), lambda g, i: (g, i, 0, 0))],
        out_shape=[jax.ShapeDtypeStruct((nh, t, LANES), F32),
                   jax.ShapeDtypeStruct((N_NSA_KV, t, n_slc), BF16),
                   jax.ShapeDtypeStruct((N_NSA_KV, nq, 8, LANES), jnp.int32)],
        compiler_params=_cparams(("parallel", "arbitrary")),
        name="nsa_cmp_select",
    )(q, kvc, imap, slopes)


def _nsa_main_kernel(flag_ref, q_ref, slc_ref, win_ref, sel_ref, oc_ref, gate_ref, slope_ref, o_ref,
                     m_ref, l_ref, acc_ref, *, n_chunks):
    g = pl.program_id(0)
    qb = pl.program_id(1)
    nq = pl.num_programs(1)
    t0 = qb * NSA_QB
    rows = NSA_HPG * NSA_QB
    q = q_ref[...].reshape(rows, LANES)
    slope = slope_ref[...]
    t = t0 + lax.broadcasted_iota(jnp.int32, (rows, 1), 0) % NSA_QB
    n_slc = sel_ref.shape[-1]

    band = WINDOW + NSA_QB
    start = pl.multiple_of(jnp.maximum(t0 - WINDOW, 0), NSA_QB)
    kw = win_ref[pl.ds(start, band), :]
    kpos = start + lax.broadcasted_iota(jnp.int32, (1, band), 1)
    dist = (t - kpos).astype(F32)
    ok = (dist >= 0.0) & (dist < float(WINDOW))
    s = jnp.where(ok, _dot_nt(q, kw) * QK_SCALE - slope * dist, NEG_BIG)
    m = jnp.max(s, axis=-1, keepdims=True)
    e = jnp.where(ok, jnp.exp(s - m), 0.0)
    o_w = _dot(e.astype(BF16), kw) / jnp.sum(e, axis=-1, keepdims=True)

    m_ref[...] = jnp.full((rows, 1), NEG_BIG, F32)
    l_ref[...] = jnp.zeros((rows, 1), F32)
    acc_ref[...] = jnp.zeros((rows, LANES), F32)
    sel = sel_ref[...]
    base = (g * nq + qb) * n_chunks

    def chunk(c, carry):
        @pl.when(flag_ref[base + c] != 0)
        def _():
            k0 = pl.multiple_of(c * SLC_CHUNK, SLC_CHUNK)
            kv = slc_ref[pl.ds(k0, SLC_CHUNK), :]
            kp = k0 + lax.broadcasted_iota(jnp.int32, (1, SLC_CHUNK), 1)
            d = (t - kp).astype(F32)
            jr = lax.broadcasted_iota(jnp.int32, (n_slc, SLC_CHUNK), 0)
            cr = lax.broadcasted_iota(jnp.int32, (n_slc, SLC_CHUNK), 1)
            expand = jnp.where(jr == c * BLK_PER_CHUNK + cr // SLC_LEN, 1.0, 0.0).astype(BF16)
            picked = _dot(sel, expand)
            picked = jnp.concatenate([picked] * NSA_HPG, axis=0)
            okc = (picked > 0.5) & (d >= 0.0)
            sc = jnp.where(okc, _dot_nt(q, kv) * QK_SCALE - slope * d, NEG_BIG)
            m_old = m_ref[...]
            m_new = jnp.maximum(m_old, jnp.max(sc, axis=-1, keepdims=True))
            pc = jnp.where(okc, jnp.exp(sc - m_new), 0.0)
            a = jnp.exp(m_old - m_new)
            l_ref[...] = a * l_ref[...] + jnp.sum(pc, axis=-1, keepdims=True)
            acc_ref[...] = a * acc_ref[...] + _dot(pc.astype(BF16), kv)
            m_ref[...] = m_new
        return carry

    lax.fori_loop(0, t0 // SLC_CHUNK + 1, chunk, 0)
    o_s = acc_ref[...] / l_ref[...]

    gt = jax.nn.sigmoid(gate_ref[...])
    o_c = oc_ref[...].reshape(rows, LANES)
    o = gt[:, 0:1] * o_c + gt[:, 1:2] * o_s + gt[:, 2:3] * o_w
    o_ref[...] = o.astype(BF16).reshape(NSA_HPG, NSA_QB, LANES)


def _nsa_main(flags, q, slc, win, sel, oc, gates, slopes):
    nh, t, _ = q.shape
    nq = t // NSA_QB
    n_slc = t // SLC_LEN
    n_chunks = t // SLC_CHUNK
    rows = NSA_HPG * NSA_QB
    grid_spec = pltpu.PrefetchScalarGridSpec(
        num_scalar_prefetch=1,
        grid=(N_NSA_KV, nq),
        in_specs=[pl.BlockSpec((NSA_HPG, NSA_QB, LANES), lambda g, i, f: (g, i, 0)),
                  pl.BlockSpec((None, t, LANES), lambda g, i, f: (g, 0, 0)),
                  pl.BlockSpec((None, t, LANES), lambda g, i, f: (g, 0, 0)),
                  pl.BlockSpec((None, NSA_QB, n_slc), lambda g, i, f: (g, i, 0)),
                  pl.BlockSpec((NSA_HPG, NSA_QB, LANES), lambda g, i, f: (g, i, 0)),
                  pl.BlockSpec((None, None, rows, 3), lambda g, i, f: (g, i, 0, 0)),
                  pl.BlockSpec((None, rows, 1), lambda g, i, f: (g, 0, 0))],
        out_specs=pl.BlockSpec((NSA_HPG, NSA_QB, LANES), lambda g, i, f: (g, i, 0)),
        scratch_shapes=[pltpu.VMEM((rows, 1), F32), pltpu.VMEM((rows, 1), F32), pltpu.VMEM((rows, LANES), F32)],
    )
    return pl.pallas_call(
        functools.partial(_nsa_main_kernel, n_chunks=n_chunks),
        grid_spec=grid_spec,
        out_shape=jax.ShapeDtypeStruct((nh, t, LANES), BF16),
        compiler_params=_cparams(("parallel", "arbitrary")),
        name="nsa_select_window",
    )(flags, q, slc, win, sel, oc, gates, slopes)


def _sb_kernel(q_ref, kv_ref, o_ref):
    qi = pl.program_id(1)
    q = q_ref[...] * QK_SCALE
    rr = lax.broadcasted_iota(jnp.int32, (SB_BLK, SB_BLK), 0)
    cc = lax.broadcasted_iota(jnp.int32, (SB_BLK, SB_BLK), 1)
    later = jnp.where(rr > cc, 1.0, 0.0).astype(BF16)
    ones = jnp.ones((SB_BLK, SB_BLK), BF16)

    def block(kb, mask, run, acc):
        kv = kv_ref[pl.ds(pl.multiple_of(kb * SB_BLK, SB_BLK), SB_BLK), :]
        z = _dot_nt(q, kv)
        sp = _softplus(z)
        log_1mb = -sp if mask is None else jnp.where(mask, -sp, 0.0)
        hi, lo = _split3(log_1mb)[:2]
        within = _dot(hi, later) + _dot(lo, later)
        total = _dot(hi, ones) + _dot(lo, ones)
        a = jnp.exp(z - sp + within + run)
        if mask is not None:
            a = jnp.where(mask, a, 0.0)
        return run + total, acc + _dot(a.astype(BF16), kv)

    zero = jnp.zeros((SB_BLK, LANES), F32)
    run, acc = block(qi, cc < rr, zero, zero)

    def cond(c):
        kb, run, _ = c
        return (kb >= 0) & (jnp.max(run) > EXP_UNDERFLOW)

    def body(c):
        kb, run, acc = c
        run, acc = block(kb, None, run, acc)
        return kb - 1, run, acc

    _, _, acc = lax.while_loop(cond, body, (qi - 1, run, acc))
    o_ref[...] = acc.astype(BF16)


def _stick_breaking(q, kv):
    nh, t, _ = q.shape
    return pl.pallas_call(
        _sb_kernel,
        grid=(nh, t // SB_BLK),
        in_specs=[pl.BlockSpec((None, SB_BLK, LANES), lambda h, i: (h, i, 0)),
                  pl.BlockSpec((None, t, LANES), lambda h, i: (h, 0, 0))],
        out_specs=pl.BlockSpec((None, SB_BLK, LANES), lambda h, i: (h, i, 0)),
        out_shape=jax.ShapeDtypeStruct((nh, t, LANES), BF16),
        compiler_params=_cparams(("parallel", "arbitrary")),
        name="stick_breaking",
    )(q, kv)


def _fox_kernel(q_ref, kv_ref, c_ref, o_ref):
    qi = pl.program_id(1)
    q = q_ref[...] * QK_SCALE

    def block(kb, masked, m, l, acc):
        k0 = pl.multiple_of(kb * FOX_BLK, FOX_BLK)
        kv = kv_ref[pl.ds(k0, FOX_BLK), :]
        s = _dot_nt(q, kv) - c_ref[:, pl.ds(k0, FOX_BLK)]
        if masked:
            rr = lax.broadcasted_iota(jnp.int32, (FOX_BLK, FOX_BLK), 0)
            cc = lax.broadcasted_iota(jnp.int32, (FOX_BLK, FOX_BLK), 1)
            s = jnp.where(cc <= rr, s, NEG_BIG)
        m_new = jnp.maximum(m, jnp.max(s, axis=-1, keepdims=True))
        p = jnp.exp(s - m_new)
        a = jnp.exp(m - m_new)
        return m_new, a * l + jnp.sum(p, axis=-1, keepdims=True), a * acc + _dot(p.astype(BF16), kv)

    init = (jnp.full((FOX_BLK, 1), NEG_BIG, F32), jnp.zeros((FOX_BLK, 1), F32), jnp.zeros((FOX_BLK, LANES), F32))
    m, l, acc = lax.fori_loop(0, qi, lambda kb, c: block(kb, False, *c), init)
    m, l, acc = block(qi, True, m, l, acc)
    o_ref[...] = (acc / l).astype(BF16)


def _forgetting(q, kv, c):
    nh, t, _ = q.shape
    blk = min(FOX_BLK, t)
    assert blk == FOX_BLK
    return pl.pallas_call(
        _fox_kernel,
        grid=(nh, t // FOX_BLK),
        in_specs=[pl.BlockSpec((None, FOX_BLK, LANES), lambda h, i: (h, i, 0)),
                  pl.BlockSpec((None, t, LANES), lambda h, i: (h, 0, 0)),
                  pl.BlockSpec((None, 1, t), lambda h, i: (h, 0, 0))],
        out_specs=pl.BlockSpec((None, FOX_BLK, LANES), lambda h, i: (h, i, 0)),
        out_shape=jax.ShapeDtypeStruct((nh, t, LANES), BF16),
        compiler_params=_cparams(("parallel", "arbitrary")),
        name="forgetting_attention",
    )(q, kv, c)


def _layer_norm(x, g, b):
    mu = jnp.mean(x, axis=-1, keepdims=True)
    xc = x - mu
    var = jnp.mean(xc * xc, axis=-1, keepdims=True)
    return xc * lax.rsqrt(var + EPS) * g + b


def _outproj_kernel(y_ref, h_ref, gn_ref, w_ref, g_ref, b_ref, o_ref):
    y = y_ref[...].astype(F32)
    gn = gn_ref[...]
    parts = []
    for lo, hi in ((0, 1024), (1024, 1536), (1536, 2048)):
        yp = y[:, lo:hi]
        ms = jnp.mean(yp * yp, axis=-1, keepdims=True)
        parts.append((yp * lax.rsqrt(ms + EPS) * gn[:, lo:hi]).astype(BF16))
    yn = jnp.concatenate(parts, axis=-1)
    mix = _dot(yn, w_ref[...])
    o_ref[...] = _layer_norm(ALPHA * h_ref[...] + mix, g_ref[...], b_ref[...])


def _outproj(y, h, gn, w, g, b):
    t, d = h.shape
    tm = min(512, t)
    row = lambda i: (i, 0)
    full = lambda i: (0, 0)
    return pl.pallas_call(
        _outproj_kernel,
        grid=(t // tm,),
        in_specs=[pl.BlockSpec((tm, d), row), pl.BlockSpec((tm, d), row), pl.BlockSpec((1, d), full),
                  pl.BlockSpec((d, d), full), pl.BlockSpec((1, d), full), pl.BlockSpec((1, d), full)],
        out_specs=pl.BlockSpec((tm, d), row),
        out_shape=jax.ShapeDtypeStruct((t, d), F32),
        compiler_params=_cparams(("parallel",)),
        name="norm_outproj_ln",
    )(y, h, gn, w, g, b)


def _moe_kernel(h_ref, rw_ref, rb_ref, wgu_ref, wd_ref, g_ref, b_ref, o_ref, xb_ref, gate_ref, acc_ref):
    e = pl.program_id(1)

    @pl.when(e == 0)
    def _():
        x = h_ref[...]
        xh, xm, _ = _split3(x)
        xb_ref[...] = xh
        w_hi = rw_ref[0]
        w_lo = rw_ref[1]
        logits = _dot(xh, w_hi) + _dot(xm, w_hi) + _dot(xh, w_lo) + rb_ref[...]
        lane = lax.broadcasted_iota(jnp.int32, logits.shape, 1)
        is_g = lane < N_GROUPS
        gl = jnp.where(is_g, logits, -jnp.inf)
        gmax = jnp.max(gl, axis=-1, keepdims=True)
        g_sel = jnp.min(jnp.where(gl == gmax, lane, LANES), axis=-1, keepdims=True)
        g_w = 1.0 / jnp.sum(jnp.where(is_g, jnp.exp(gl - gmax), 0.0), axis=-1, keepdims=True)
        ex = lane - N_GROUPS
        in_grp = (ex >= 0) & (ex < N_EXPERTS) & (ex // EXPERTS_PER_GROUP == g_sel)
        el = jnp.where(in_grp, logits, -jnp.inf)
        v1 = jnp.max(el, axis=-1, keepdims=True)
        i1 = jnp.min(jnp.where(el == v1, lane, LANES), axis=-1, keepdims=True)
        el2 = jnp.where(lane == i1, -jnp.inf, el)
        v2 = jnp.max(el2, axis=-1, keepdims=True)
        i2 = jnp.min(jnp.where(el2 == v2, lane, LANES), axis=-1, keepdims=True)
        e2 = jnp.exp(v2 - v1)
        w1 = g_w / (1.0 + e2)
        w2 = g_w * e2 / (1.0 + e2)
        gate_ref[...] = jnp.where(lane == i1, w1, 0.0) + jnp.where(lane == i2, w2, 0.0)
        acc_ref[...] = jnp.zeros_like(acc_ref)

    xb = xb_ref[...]
    gu = _dot(xb, wgu_ref[...])
    lane = lax.broadcasted_iota(jnp.int32, gate_ref.shape, 1)
    gcol = jnp.sum(jnp.where(lane == e + N_GROUPS, gate_ref[...], 0.0), axis=-1, keepdims=True)
    hid = jax.nn.silu(gu[:, :D_FF]) * gu[:, D_FF:]
    acc_ref[...] += _dot((hid * gcol).astype(BF16), wd_ref[...])

    @pl.when(e == N_EXPERTS - 1)
    def _():
        o_ref[...] = _layer_norm(ALPHA * h_ref[...] + acc_ref[...], g_ref[...], b_ref[...])


def _moe(h, rw, rb, wgu, wd, g, b):
    t, d = h.shape
    tm = min(512, t)
    return pl.pallas_call(
        _moe_kernel,
        grid=(t // tm, N_EXPERTS),
        in_specs=[pl.BlockSpec((tm, d), lambda i, e: (i, 0)),
                  pl.BlockSpec((2, d, LANES), lambda i, e: (0, 0, 0)),
                  pl.BlockSpec((1, LANES), lambda i, e: (0, 0)),
                  pl.BlockSpec((None, d, 2 * D_FF), lambda i, e: (e, 0, 0)),
                  pl.BlockSpec((None, D_FF, d), lambda i, e: (e, 0, 0)),
                  pl.BlockSpec((1, d), lambda i, e: (0, 0)),
                  pl.BlockSpec((1, d), lambda i, e: (0, 0))],
        out_specs=pl.BlockSpec((tm, d), lambda i, e: (i, 0)),
        out_shape=jax.ShapeDtypeStruct((t, d), F32),
        scratch_shapes=[pltpu.VMEM((tm, d), BF16), pltpu.VMEM((tm, LANES), F32), pltpu.VMEM((tm, d), F32)],
        compiler_params=_cparams(("parallel", "arbitrary")),
        name="moe_ln",
    )(h, rw, rb, wgu, wd, g, b)


def _cmp_to_slc_matrix(n_cmp_pad, n_slc):
    r, c = SLC_LEN // CMP_STRIDE, CMP_LEN // CMP_STRIDE
    m = np.zeros((n_cmp_pad, n_slc), np.float32)
    for j in range(n_slc):
        for a in range(r):
            for bb in range(c):
                i = r * j + a + bb
                if i < n_cmp_pad - 1:
                    m[i, j] += 1.0
    return m


def _heads(a, n, width):
    t = a.shape[0]
    a = a.reshape(t, n, width)
    if width < LANES:
        a = jnp.pad(a, ((0, 0), (0, 0), (0, LANES - width)))
    return a.transpose(1, 0, 2)


def _unheads(o):
    n, t, _ = o.shape
    return o[:, :, HEAD_DIM:].transpose(1, 0, 2).reshape(t, n * HEAD_DIM)


def _compress_weights(pos, w1, w2):
    eye = jnp.eye(N_NSA_KV, dtype=F32)
    half = CMP_LEN // 2
    cols = N_NSA_KV * HEAD_DIM

    def expand(w):
        return jnp.einsum('lde,gh->lgdhe', w, eye).reshape(half * cols, cols).astype(BF16)

    def tile_pos(p):
        return jnp.broadcast_to(p[:, None, :], (half, N_NSA_KV, HEAD_DIM)).reshape(1, half * cols)

    pos2 = jnp.concatenate([tile_pos(pos[:half]), tile_pos(pos[half:])], axis=0)
    w2bd = jnp.einsum('ef,gh->gehf', w2, eye).reshape(cols, cols).astype(BF16)
    return pos2, expand(w1[:half]), expand(w1[half:]), w2bd


def _mixer(h, w_in, cmp_pos_k, cmp_w1_k, cmp_w2_k, cmp_pos_v, cmp_w1_v, cmp_w2_v, fox_forget_bias,
           norm_nsa, norm_sb, norm_fox, w_out, ln_g, ln_b):
    t = h.shape[0]
    nq = t // NSA_QB
    w_main = w_in[:, _PERM_MAIN].astype(BF16)
    w_tail = jnp.pad(w_in[:, _PERM_TAIL], ((0, 0), (0, LANES - _PERM_TAIL.size))).astype(BF16)
    main, tail = _inproj(h, w_main, w_tail)

    q_nsa = _heads(main[:, 0:1024], N_NSA_HEADS, HEAD_DIM)
    n_rows = t // CMP_STRIDE
    kc = _compress(main[:, 1024:1280].reshape(n_rows, -1), *_compress_weights(cmp_pos_k, cmp_w1_k, cmp_w2_k))
    vc = _compress(main[:, 1280:1536].reshape(n_rows, -1), *_compress_weights(cmp_pos_v, cmp_w1_v, cmp_w2_v))
    kvc = jnp.concatenate([kc.reshape(n_rows, N_NSA_KV, HEAD_DIM), vc.reshape(n_rows, N_NSA_KV, HEAD_DIM)],
                          axis=-1).transpose(1, 0, 2)
    slc = _heads(main[:, 1536:2048], N_NSA_KV, LANES)
    win = _heads(main[:, 2048:2560], N_NSA_KV, LANES)
    slopes = 2.0 ** (-8.0 * jnp.arange(1, N_NSA_HEADS + 1, dtype=F32) / N_NSA_HEADS)
    slopes = jnp.repeat(slopes.reshape(N_NSA_KV, NSA_HPG), NSA_QB, axis=1)[..., None]
    imap = jnp.asarray(_cmp_to_slc_matrix(n_rows, t // SLC_LEN), BF16)
    oc, sel, flags = _nsa_cmp(q_nsa, kvc, imap, slopes)
    flags = flags[:, :, 0, :t // SLC_CHUNK].reshape(-1)
    gates = tail[:, :48].reshape(nq, NSA_QB, N_NSA_KV, NSA_HPG, 3).transpose(2, 0, 3, 1, 4)
    gates = gates.reshape(N_NSA_KV, nq, NSA_HPG * NSA_QB, 3)
    o_nsa = _unheads(_nsa_main(flags, q_nsa, slc, win, sel, oc, gates, slopes))

    o_sb = _unheads(_stick_breaking(_heads(main[:, 2560:3072], N_SB_HEADS, HEAD_DIM),
                                    _heads(main[:, 3072:4096], N_SB_HEADS, LANES)))

    nb = t // LANES
    ff_rows = tail[:, 48:56].T.reshape(N_FOX_HEADS * nb, LANES)
    bias_rows = jnp.repeat(fox_forget_bias, nb)[:, None]
    c = _forget_cumsum(ff_rows, bias_rows, nb).reshape(N_FOX_HEADS, 1, t)
    o_fox = _unheads(_forgetting(_heads(main[:, 4096:4608], N_FOX_HEADS, HEAD_DIM),
                                 _heads(main[:, 4608:5632], N_FOX_HEADS, LANES), c))

    y = jnp.concatenate([o_nsa, o_sb, o_fox], axis=-1)
    gn = jnp.concatenate([norm_nsa, norm_sb, norm_fox])[None, :]
    return _outproj(y, h, gn, w_out.astype(BF16), ln_g[None, :], ln_b[None, :])


def _ffn(h, rg_w, rg_b, re_w, re_b, w_gate, w_up, w_down, ln_g, ln_b):
    rw = jnp.pad(jnp.concatenate([rg_w, re_w], axis=1), ((0, 0), (0, LANES - N_GROUPS - N_EXPERTS)))
    rw_hi = rw.astype(BF16)
    rw_lo = (rw - rw_hi.astype(F32)).astype(BF16)
    rb = jnp.pad(jnp.concatenate([rg_b, re_b]), (0, LANES - N_GROUPS - N_EXPERTS))[None, :]
    wgu = jnp.concatenate([w_gate, w_up], axis=-1).astype(BF16)
    return _moe(h, jnp.stack([rw_hi, rw_lo]), rb, wgu, w_down.astype(BF16), ln_g[None, :], ln_b[None, :])


def kernel(x, w_in, cmp_pos_k, cmp_w1_k, cmp_w2_k, cmp_pos_v, cmp_w1_v, cmp_w2_v, fox_forget_bias, norm_nsa, norm_sb, norm_fox, w_out, ln1_g, ln1_b, router_group_w, router_group_b, router_expert_w, router_expert_b, expert_w_gate, expert_w_up, expert_w_down, ln2_g, ln2_b):
    b, t, d = x.shape
    assert b == 1 and d == D_MODEL and t % 1024 == 0
    h = x.reshape(t, d)
    for l in range(DEPTH):
        h = _mixer(h, w_in[l], cmp_pos_k[l], cmp_w1_k[l], cmp_w2_k[l], cmp_pos_v[l], cmp_w1_v[l], cmp_w2_v[l],
                   fox_forget_bias[l], norm_nsa[l], norm_sb[l], norm_fox[l], w_out[l], ln1_g[l], ln1_b[l])
        h = _ffn(h, router_group_w[l], router_group_b[l], router_expert_w[l], router_expert_b[l],
                 expert_w_gate[l], expert_w_up[l], expert_w_down[l], ln2_g[l], ln2_b[l])
    return h.reshape(b, t, d)
```

```python
import functools

import numpy as np
import jax
import jax.numpy as jnp
from jax import lax
from jax.experimental import pallas as pl
from jax.experimental.pallas import tpu as pltpu

F32 = jnp.float32
BF16 = jnp.bfloat16

D_MODEL = 2048
DEPTH = 2
HEAD_DIM = 64
N_NSA_HEADS = 16
N_NSA_KV = 4
NSA_HPG = 4
N_SB_HEADS = 8
N_FOX_HEADS = 8
CMP_LEN = 32
CMP_STRIDE = 16
SLC_LEN = 64
SLC_TOPK = 8
WINDOW = 512
N_GROUPS = 4
EXPERTS_PER_GROUP = 4
N_EXPERTS = 16
D_FF = D_MODEL // 8
ALPHA = (2 * DEPTH) ** 0.25
EPS = 1e-5
NEG_BIG = -1e30
SEL_BONUS = 1e6
QK_SCALE = HEAD_DIM ** -0.5

LANES = 128
NSA_QB = 128
SLC_CHUNK = 256
BLK_PER_CHUNK = SLC_CHUNK // SLC_LEN
SB_BLK = 256
FOX_TQ = 512
FOX_TK = 256
EXP_UNDERFLOW = -104.0
BOUND_SLACK = 6.0
SLC_BLK0, WIN_BLK0 = 1536 // 128, 2048 // 128
SB_Q_BLK0, SB_KV_BLK0 = 2560 // 128, 3072 // 128
FOX_Q_BLK0, FOX_KV_BLK0 = 4096 // 128, 4608 // 128
MAIN_W = 5632
VMEM_LIMIT = 56 * 1024 * 1024

_PROJ_SIZES = (1024, 256, 256, 256, 256, 256, 256, 48, 512, 512, 512, 512, 512, 512, 8)
_OFF = np.concatenate([[0], np.cumsum(_PROJ_SIZES)])


def _perm_columns():
    seg = lambda i: np.arange(_OFF[i], _OFF[i + 1])
    nq, ck, cv, sk, sv, wk, wv, ng, sbq, sbk, sbv, fq, fk, fv, ff = [seg(i) for i in range(15)]

    def interleave(k, v, n, swap_odd=False):
        out = []
        for h in range(n):
            kh, vh = k[h * 64:(h + 1) * 64], v[h * 64:(h + 1) * 64]
            out.append(np.concatenate([vh, kh] if (swap_odd and h % 2) else [kh, vh]))
        return np.concatenate(out)

    main = np.concatenate([nq, ck, cv, interleave(sk, sv, 4), interleave(wk, wv, 4), sbq,
                           interleave(sbk, sbv, 8, True), fq, interleave(fk, fv, 8, True)])
    tail = np.concatenate([ng, ff])
    assert main.size == MAIN_W
    return main, tail


_PERM_MAIN, _PERM_TAIL = _perm_columns()


def _perm_mixer_columns():
    def pairs(base, n_heads):
        order = [h for p in range(n_heads // 2) for h in (2 * p + 1, 2 * p)]
        return np.concatenate([base + h * HEAD_DIM + np.arange(HEAD_DIM) for h in order])

    nsa_w = N_NSA_HEADS * HEAD_DIM
    sb_w = N_SB_HEADS * HEAD_DIM
    return np.concatenate([pairs(0, N_NSA_HEADS), pairs(nsa_w, N_SB_HEADS), pairs(nsa_w + sb_w, N_FOX_HEADS)])


_PERM_MIX = _perm_mixer_columns()


def _cparams(sem):
    return pltpu.CompilerParams(dimension_semantics=sem, vmem_limit_bytes=VMEM_LIMIT)


def _dot(a, b):
    return jnp.dot(a, b, preferred_element_type=F32)


def _dot_nt(a, b):
    return lax.dot_general(a, b, (((1,), (1,)), ((), ())), preferred_element_type=F32)


def _split3(x):
    hi = x.astype(BF16)
    r = x - hi.astype(F32)
    mid = r.astype(BF16)
    lo = (r - mid.astype(F32)).astype(BF16)
    return hi, mid, lo


def _dot_split(x, w, parts):
    pieces = _split3(x)[:parts]
    out = _dot(pieces[0], w)
    for p in pieces[1:]:
        out = out + _dot(p, w)
    return out


def _softplus(z):
    return jnp.maximum(z, 0.0) + jnp.log1p(jnp.exp(-jnp.abs(z)))


def _inproj_kernel(x_ref, w_ref, wt_ref, o_ref, ot_ref, xb_ref):
    @pl.when(pl.program_id(1) == 0)
    def _():
        xb = x_ref[...].astype(BF16)
        xb_ref[...] = xb
        ot_ref[...] = _dot(xb, wt_ref[...])

    o_ref[...] = _dot(xb_ref[...], w_ref[...]).astype(BF16)


def _inproj(h, w_main, w_tail):
    t, d = h.shape
    tm, tn = min(1024, t), 512
    return pl.pallas_call(
        _inproj_kernel,
        grid=(t // tm, MAIN_W // tn),
        in_specs=[pl.BlockSpec((tm, d), lambda i, j: (i, 0)),
                  pl.BlockSpec((d, tn), lambda i, j: (0, j)),
                  pl.BlockSpec((d, LANES), lambda i, j: (0, 0))],
        out_specs=[pl.BlockSpec((tm, tn), lambda i, j: (i, j)),
                   pl.BlockSpec((tm, LANES), lambda i, j: (i, 0))],
        out_shape=[jax.ShapeDtypeStruct((t, MAIN_W), BF16), jax.ShapeDtypeStruct((t, LANES), F32)],
        scratch_shapes=[pltpu.VMEM((tm, d), BF16)],
        compiler_params=_cparams(("parallel", "arbitrary")),
        name="inproj",
    )(h, w_main, w_tail)


def _cumsum_kernel(ff_ref, b_ref, c_ref, *, nb):
    x = ff_ref[...] + b_ref[...]
    lf = -_softplus(-x)
    r = lf.shape[0]
    jj = lax.broadcasted_iota(jnp.int32, (LANES, LANES), 0)
    kk = lax.broadcasted_iota(jnp.int32, (LANES, LANES), 1)
    upper = jnp.where(jj <= kk, 1.0, 0.0).astype(BF16)
    ones = jnp.ones((LANES, LANES), BF16)
    within = _dot_split(lf, upper, 3)
    tot = _dot_split(lf, ones, 3)
    rr = lax.broadcasted_iota(jnp.int32, (r, r), 0)
    cc = lax.broadcasted_iota(jnp.int32, (r, r), 1)
    earlier = jnp.where((cc < rr) & (cc // nb == rr // nb), 1.0, 0.0).astype(BF16)
    hi, mid, lo = _split3(tot)
    off = _dot(earlier, hi) + _dot(earlier, mid) + _dot(earlier, lo)
    c_ref[...] = within + off


def _forget_cumsum(ff_rows, bias_rows, nb):
    r = ff_rows.shape[0]
    return pl.pallas_call(
        functools.partial(_cumsum_kernel, nb=nb),
        out_shape=jax.ShapeDtypeStruct((r, LANES), F32),
        compiler_params=pltpu.CompilerParams(vmem_limit_bytes=VMEM_LIMIT),
        name="forget_cumsum",
    )(ff_rows, bias_rows)


def _gelu_tanh(x):
    return 0.5 * x * (1.0 + jnp.tanh(np.sqrt(2.0 / np.pi).astype(np.float32) * (x + 0.044715 * (x * x * x))))


def _compress_kernel(x_ref, pos_ref, wa_ref, wb_ref, w2_ref, o_ref):
    x = x_ref[...].astype(F32)
    n = x.shape[0]
    first = _dot((x + pos_ref[0:1, :]).astype(BF16), wa_ref[...])
    second = _dot((x + pos_ref[1:2, :]).astype(BF16), wb_ref[...])
    pre = first + pltpu.roll(second, n - 1, 0)
    o_ref[...] = _dot(_gelu_tanh(pre).astype(BF16), w2_ref[...]).astype(BF16)


def _compress(x2, pos2, wa, wb, w2):
    n = x2.shape[0]
    return pl.pallas_call(
        _compress_kernel,
        out_shape=jax.ShapeDtypeStruct((n, N_NSA_KV * HEAD_DIM), BF16),
        compiler_params=pltpu.CompilerParams(vmem_limit_bytes=VMEM_LIMIT),
        name="nsa_compress",
    )(x2, pos2, wa, wb, w2)


def _group_queries(q_ref):
    q = q_ref[...]
    low = lax.broadcasted_iota(jnp.int32, (NSA_QB, LANES), 1) < HEAD_DIM
    zero = jnp.zeros((NSA_QB, LANES), BF16)
    out = []
    for j in range(NSA_HPG // 2):
        pair = q[:, j * LANES:(j + 1) * LANES]
        swapped = pltpu.roll(pair.astype(F32), HEAD_DIM, 1).astype(BF16)
        out += [jnp.where(low, pair, zero), jnp.where(low, swapped, zero)]
    return jnp.concatenate(out, axis=0) * QK_SCALE


def _head_sum(p):
    return p[0:NSA_QB] + p[NSA_QB:2 * NSA_QB] + p[2 * NSA_QB:3 * NSA_QB] + p[3 * NSA_QB:4 * NSA_QB]


def _nsa_cmp_kernel(q_ref, kvc_ref, imap_ref, slope_ref, oc_ref, sel_ref, flag_ref, s_ref, *, chunk):
    t0 = pl.program_id(1) * NSA_QB
    rows = NSA_HPG * NSA_QB
    q = _group_queries(q_ref)
    slope = slope_ref[...]
    n_slc = sel_ref.shape[-1]
    trow = lax.broadcasted_iota(jnp.int32, (rows, 1), 0) % NSA_QB
    n_valid = t0 // CMP_STRIDE + (NSA_QB - CMP_LEN) // CMP_STRIDE + 1
    n_steps = (n_valid + chunk - 1) // chunk
    col = lax.broadcasted_iota(jnp.int32, (1, chunk), 1)

    def logits(c, carry):
        m, l = carry
        n0 = pl.multiple_of(c * chunk, chunk)
        end_rel = (n0 + col) * CMP_STRIDE + (CMP_LEN - 1) - t0
        dist = trow - end_rel
        mask = dist >= 0
        s = jnp.where(mask, _dot_nt(q, kvc_ref[pl.ds(n0, chunk), :]) - slope * dist.astype(F32), NEG_BIG)
        s_ref[:, pl.ds(n0, chunk)] = s
        m_new = jnp.maximum(m, jnp.max(s, axis=-1, keepdims=True))
        e = jnp.where(mask, jnp.exp(s - m_new), 0.0)
        return m_new, l * jnp.exp(m - m_new) + jnp.sum(e, axis=-1, keepdims=True)

    m, l = lax.fori_loop(0, n_steps, logits,
                         (jnp.full((rows, 1), NEG_BIG, F32), jnp.zeros((rows, 1), F32)))
    inv = 1.0 / jnp.where(l > 0.0, l, 1.0)

    def weights(c, carry):
        acc, imp = carry
        n0 = pl.multiple_of(c * chunk, chunk)
        s = s_ref[:, pl.ds(n0, chunk)]
        p = jnp.where(s > 0.5 * NEG_BIG, jnp.exp(s - m), 0.0) * inv
        acc = acc + _dot(p.astype(BF16), kvc_ref[pl.ds(n0, chunk), :])
        return acc, imp + _dot_split(_head_sum(p), imap_ref[pl.ds(n0, chunk), :], 2)

    acc, imp = lax.fori_loop(0, n_steps, weights,
                             (jnp.zeros((rows, LANES), F32), jnp.zeros((NSA_QB, n_slc), F32)))
    oc_ref[...] = acc.reshape(NSA_HPG, NSA_QB, LANES)

    tq = t0 + lax.broadcasted_iota(jnp.int32, (1, NSA_QB), 1)
    jb = lax.broadcasted_iota(jnp.int32, (n_slc, 1), 0)
    jbf = jb.astype(F32)
    cur = tq // SLC_LEN
    forced = (jb == 0) | (jb == cur) | (jb == cur - 1)
    valid = jb * SLC_LEN <= tq
    score = jnp.where(valid, imp.T + jnp.where(forced, SEL_BONUS, 0.0), -jnp.inf)
    sel = jnp.zeros((n_slc, NSA_QB), F32)
    for _ in range(SLC_TOPK):
        best = jnp.max(score, axis=0, keepdims=True)
        first = jnp.min(jnp.where(score == best, jbf, float(n_slc)), axis=0, keepdims=True)
        hit = jbf == first
        sel = jnp.where(hit, 1.0, sel)
        score = jnp.where(hit, -jnp.inf, score)
    sel = jnp.where(valid, sel, 0.0).T
    sel_ref[...] = sel.astype(BF16)
    any_q = jnp.max(sel, axis=0, keepdims=True)
    jr = lax.broadcasted_iota(jnp.int32, (n_slc, LANES), 0)
    cr = lax.broadcasted_iota(jnp.int32, (n_slc, LANES), 1)
    group = jnp.where(jr // BLK_PER_CHUNK == cr, 1.0, 0.0).astype(BF16)
    per_chunk = _dot(jnp.broadcast_to(any_q, (8, n_slc)).astype(BF16), group)
    flag_ref[...] = jnp.where(per_chunk > 0.5, 1, 0).astype(jnp.int32)


def _nsa_cmp(main, kvc, imap, slopes):
    t = main.shape[0]
    nq = t // NSA_QB
    n = kvc.shape[1]
    n_slc = t // SLC_LEN
    rows = NSA_HPG * NSA_QB
    chunk = min(256, n)
    return pl.pallas_call(
        functools.partial(_nsa_cmp_kernel, chunk=chunk),
        grid=(N_NSA_KV, nq),
        in_specs=[pl.BlockSpec((NSA_QB, NSA_HPG * HEAD_DIM), lambda g, i: (i, g)),
                  pl.BlockSpec((None, n, LANES), lambda g, i: (g, 0, 0)),
                  pl.BlockSpec((n, n_slc), lambda g, i: (0, 0)),
                  pl.BlockSpec((None, rows, 1), lambda g, i: (g, 0, 0))],
        out_specs=[pl.BlockSpec((NSA_HPG, NSA_QB, LANES), lambda g, i: (g, i, 0)),
                   pl.BlockSpec((None, NSA_QB, n_slc), lambda g, i: (g, i, 0)),
                   pl.BlockSpec((None, None, 8, LANES), lambda g, i: (g, i, 0, 0))],
        out_shape=[jax.ShapeDtypeStruct((N_NSA_HEADS, t, LANES), F32),
                   jax.ShapeDtypeStruct((N_NSA_KV, t, n_slc), BF16),
                   jax.ShapeDtypeStruct((N_NSA_KV, nq, 8, LANES), jnp.int32)],
        scratch_shapes=[pltpu.VMEM((rows, n), F32)],
        compiler_params=_cparams(("parallel", "arbitrary")),
        name="nsa_cmp_select",
    )(main, kvc, imap, slopes)


def _nsa_main_kernel(idx_ref, cnt_ref, q_ref, slc_ref, win_ref, sel_ref, oc_ref, gate_ref, slope_ref, o_ref,
                     *, n_chunks):
    g = pl.program_id(0)
    qb = pl.program_id(1)
    nq = pl.num_programs(1)
    t0 = qb * NSA_QB
    rows = NSA_HPG * NSA_QB
    q = _group_queries(q_ref)
    slope = slope_ref[...]
    trow = lax.broadcasted_iota(jnp.int32, (rows, 1), 0) % NSA_QB
    n_slc = sel_ref.shape[-1]
    sel = sel_ref[...]

    band = WINDOW + NSA_QB
    start = pl.multiple_of(jnp.maximum(t0 - WINDOW, 0), NSA_QB)
    kw = win_ref[pl.ds(start, band), :]
    krel = (start - t0) + lax.broadcasted_iota(jnp.int32, (1, band), 1)
    dist = trow - krel
    ok = (dist >= 0) & (dist < WINDOW)
    s = jnp.where(ok, _dot_nt(q, kw) + slope * krel.astype(F32), NEG_BIG)
    m = jnp.max(s, axis=-1, keepdims=True)
    e = jnp.where(ok, jnp.exp(s - m), 0.0)
    o_w = _dot(e.astype(BF16), kw) / jnp.sum(e, axis=-1, keepdims=True)

    def picked_keys(block_of_key):
        jr = lax.broadcasted_iota(jnp.int32, (n_slc, block_of_key.shape[1]), 0)
        expand = jnp.where(jr == block_of_key, 1.0, 0.0).astype(BF16)
        return jnp.concatenate([_dot(sel, expand)] * NSA_HPG, axis=0)

    cd = t0 // SLC_CHUNK
    k0 = pl.multiple_of(cd * SLC_CHUNK, SLC_CHUNK)
    kv = slc_ref[pl.ds(k0, SLC_CHUNK), :]
    cr = lax.broadcasted_iota(jnp.int32, (1, SLC_CHUNK), 1)
    krel = (k0 - t0) + cr
    okc = (picked_keys(cd * BLK_PER_CHUNK + cr // SLC_LEN) > 0.5) & (krel <= trow)
    s = jnp.where(okc, _dot_nt(q, kv) + slope * krel.astype(F32), NEG_BIG)
    m = jnp.max(s, axis=-1, keepdims=True)
    p = jnp.where(okc, jnp.exp(s - m), 0.0)
    l = jnp.sum(p, axis=-1, keepdims=True)
    acc = _dot(p.astype(BF16), kv)

    base = (g * nq + qb) * n_chunks
    cnt = cnt_ref[g * nq + qb]
    cr2 = lax.broadcasted_iota(jnp.int32, (1, 2 * SLC_CHUNK), 1)
    second = cr2 >= SLC_CHUNK

    def two_chunks(i, carry):
        m, l, acc = carry
        c1 = idx_ref[base + 2 * i]
        has2 = 2 * i + 1 < cnt
        c2 = jnp.where(has2, idx_ref[base + 2 * i + 1], c1)
        kv = jnp.concatenate([slc_ref[pl.ds(pl.multiple_of(c1 * SLC_CHUNK, SLC_CHUNK), SLC_CHUNK), :],
                              slc_ref[pl.ds(pl.multiple_of(c2 * SLC_CHUNK, SLC_CHUNK), SLC_CHUNK), :]], axis=0)
        blk2 = jnp.where(has2, (c2 - 1) * BLK_PER_CHUNK, -4 * n_slc)
        block_of_key = jnp.where(second, blk2, c1 * BLK_PER_CHUNK) + cr2 // SLC_LEN
        krel = (jnp.where(second, (c2 - 1) * SLC_CHUNK, c1 * SLC_CHUNK) - t0 + cr2).astype(F32)
        s = _dot_nt(q, kv) + slope * krel + (picked_keys(block_of_key) - 1.0) * (-NEG_BIG)
        m_new = jnp.maximum(m, jnp.max(s, axis=-1, keepdims=True))
        p = jnp.exp(s - m_new)
        a = jnp.exp(m - m_new)
        return m_new, a * l + jnp.sum(p, axis=-1, keepdims=True), a * acc + _dot(p.astype(BF16), kv)

    m, l, acc = lax.fori_loop(0, (cnt + 1) // 2, two_chunks, (m, l, acc))
    o_s = acc / l

    gt = jax.nn.sigmoid(gate_ref[...])
    o_c = oc_ref[...].reshape(rows, LANES)
    o = gt[:, 0:1] * o_c + gt[:, 1:2] * o_s + gt[:, 2:3] * o_w
    low = lax.broadcasted_iota(jnp.int32, (NSA_QB, LANES), 1) < HEAD_DIM
    pairs = []
    for j in range(NSA_HPG // 2):
        even = o[2 * j * NSA_QB:(2 * j + 1) * NSA_QB]
        odd = o[(2 * j + 1) * NSA_QB:(2 * j + 2) * NSA_QB]
        pairs.append(jnp.where(low, pltpu.roll(odd, HEAD_DIM, 1), even))
    o_ref[...] = jnp.concatenate(pairs, axis=1).astype(BF16)


def _nsa_main(idx, cnt, main, sel, oc, gates, slopes):
    t = main.shape[0]
    nq = t // NSA_QB
    n_slc = t // SLC_LEN
    n_chunks = t // SLC_CHUNK
    rows = NSA_HPG * NSA_QB
    width = NSA_HPG * HEAD_DIM
    grid_spec = pltpu.PrefetchScalarGridSpec(
        num_scalar_prefetch=2,
        grid=(N_NSA_KV, nq),
        in_specs=[pl.BlockSpec((NSA_QB, width), lambda g, i, ix, ct: (i, g)),
                  pl.BlockSpec((t, LANES), lambda g, i, ix, ct: (0, SLC_BLK0 + g)),
                  pl.BlockSpec((t, LANES), lambda g, i, ix, ct: (0, WIN_BLK0 + g)),
                  pl.BlockSpec((None, NSA_QB, n_slc), lambda g, i, ix, ct: (g, i, 0)),
                  pl.BlockSpec((NSA_HPG, NSA_QB, LANES), lambda g, i, ix, ct: (g, i, 0)),
                  pl.BlockSpec((None, None, rows, 3), lambda g, i, ix, ct: (g, i, 0, 0)),
                  pl.BlockSpec((None, rows, 1), lambda g, i, ix, ct: (g, 0, 0))],
        out_specs=pl.BlockSpec((NSA_QB, width), lambda g, i, ix, ct: (i, g)),
    )
    return pl.pallas_call(
        functools.partial(_nsa_main_kernel, n_chunks=n_chunks),
        grid_spec=grid_spec,
        out_shape=jax.ShapeDtypeStruct((t, N_NSA_HEADS * HEAD_DIM), BF16),
        compiler_params=_cparams(("parallel", "arbitrary")),
        name="nsa_select_window",
    )(idx, cnt, main, main, main, sel, oc, gates, slopes)


def _pair_queries(q_ref):
    q = q_ref[...]
    low = lax.broadcasted_iota(jnp.int32, q.shape, 1) < HEAD_DIM
    zero = jnp.zeros_like(q)
    return jnp.where(low, q, zero) * QK_SCALE, jnp.where(low, zero, q) * QK_SCALE, low


def _sb_kernel(q_ref, kve_ref, kvo_ref, o_ref):
    qi = pl.program_id(1)
    q_e, q_o, low = _pair_queries(q_ref)
    rr = lax.broadcasted_iota(jnp.int32, (SB_BLK, SB_BLK), 0)
    cc = lax.broadcasted_iota(jnp.int32, (SB_BLK, SB_BLK), 1)
    later = jnp.where(rr > cc, 1.0, 0.0).astype(BF16)
    ones = jnp.ones((SB_BLK, LANES), BF16)

    def head_block(q, kv_ref, kb, mask, run, acc):
        kv = kv_ref[pl.ds(pl.multiple_of(kb * SB_BLK, SB_BLK), SB_BLK), :]
        z = _dot_nt(q, kv)
        sp = _softplus(z)
        log_1mb = -sp if mask is None else jnp.where(mask, -sp, 0.0)
        hi, lo = _split3(log_1mb)[:2]
        within = _dot(hi, later) + _dot(lo, later)
        total = _dot(hi, ones) + _dot(lo, ones)
        a = jnp.exp(z - sp + within + run[:, 0:1])
        if mask is not None:
            a = jnp.where(mask, a, 0.0)
        return run + total, acc + _dot(a.astype(BF16), kv)

    def both(kb, mask, st):
        run_e, acc_e, run_o, acc_o = st
        run_e, acc_e = head_block(q_e, kve_ref, kb, mask, run_e, acc_e)
        run_o, acc_o = head_block(q_o, kvo_ref, kb, mask, run_o, acc_o)
        return run_e, acc_e, run_o, acc_o

    def alive(st):
        return (jnp.max(jnp.maximum(st[0], st[2])) > EXP_UNDERFLOW).astype(jnp.int32)

    zero = jnp.zeros((SB_BLK, LANES), F32)
    st = both(qi, cc < rr, (zero, zero, zero, zero))

    def cond(c):
        return (c[0] >= 0) & (c[1] > 0)

    def body(c):
        st = both(c[0], None, c[2])
        return c[0] - 1, alive(st), st

    _, _, st = lax.while_loop(cond, body, (qi - 1, alive(st), st))
    o_ref[...] = jnp.where(low, st[3], st[1]).astype(BF16)


def _stick_breaking(main):
    t = main.shape[0]
    return pl.pallas_call(
        _sb_kernel,
        grid=(N_SB_HEADS // 2, t // SB_BLK),
        in_specs=[pl.BlockSpec((SB_BLK, LANES), lambda p, i: (i, SB_Q_BLK0 + p)),
                  pl.BlockSpec((t, LANES), lambda p, i: (0, SB_KV_BLK0 + 2 * p)),
                  pl.BlockSpec((t, LANES), lambda p, i: (0, SB_KV_BLK0 + 2 * p + 1))],
        out_specs=pl.BlockSpec((SB_BLK, LANES), lambda p, i: (i, p)),
        out_shape=jax.ShapeDtypeStruct((t, N_SB_HEADS * HEAD_DIM), BF16),
        compiler_params=_cparams(("parallel", "arbitrary")),
        name="stick_breaking",
    )(main, main, main)


def _fox_kernel(cend_ref, q_ref, kve_ref, kvo_ref, c_ref, o_ref, kmax_ref):
    pair = pl.program_id(0)
    qi = pl.program_id(1)
    ratio = FOX_TQ // FOX_TK
    nkb = pl.num_programs(1) * ratio
    t = kve_ref.shape[0]
    scan = 1024

    @pl.when(qi == 0)
    def _():
        def max_norm(ref, keys_low):
            keep = (lax.broadcasted_iota(jnp.int32, (scan, LANES), 1) < HEAD_DIM) == keys_low

            def step(r, best):
                k = ref[pl.ds(pl.multiple_of(r * scan, scan), scan), :].astype(F32)
                k = jnp.where(keep, k, 0.0)
                return jnp.maximum(best, jnp.sum(k * k, axis=-1, keepdims=True))

            best = lax.fori_loop(0, t // scan, step, jnp.zeros((scan, 1), F32))
            return jnp.sqrt(jnp.max(best, axis=0, keepdims=True))

        kmax_ref[0:1, :] = jnp.broadcast_to(max_norm(kve_ref, True), (1, LANES))
        kmax_ref[1:2, :] = jnp.broadcast_to(max_norm(kvo_ref, False), (1, LANES))

    q_e, q_o, low = _pair_queries(q_ref)

    def logit_bound(q, row):
        qf = q.astype(F32)
        return jnp.sqrt(jnp.sum(qf * qf, axis=-1, keepdims=True)) * kmax_ref[row:row + 1, 0:1]

    bound = (logit_bound(q_e, 0), logit_bound(q_o, 1))
    rr = lax.broadcasted_iota(jnp.int32, (FOX_TQ, FOX_TK), 0)
    cc = lax.broadcasted_iota(jnp.int32, (FOX_TQ, FOX_TK), 1)

    def head_block(q, kv_ref, row, kb, mask, st):
        m, l, acc = st
        k0 = pl.multiple_of(kb * FOX_TK, FOX_TK)
        kv = kv_ref[pl.ds(k0, FOX_TK), :]
        s = _dot_nt(q, kv) - c_ref[row:row + 1, pl.ds(k0, FOX_TK)]
        if mask is not None:
            s = jnp.where(mask, s, NEG_BIG)
        m_new = jnp.maximum(m, jnp.max(s, axis=-1, keepdims=True))
        p = jnp.exp(s - m_new)
        if mask is not None:
            p = jnp.where(mask, p, 0.0)
        a = jnp.exp(m - m_new)
        return m_new, a * l + jnp.sum(p, axis=-1, keepdims=True), a * acc + _dot(p.astype(BF16), kv)

    def both(kb, mask, st):
        return (head_block(q_e, kve_ref, 0, kb, mask, st[0]), head_block(q_o, kvo_ref, 1, kb, mask, st[1]))

    def needed(kb, st):
        kb = jnp.maximum(kb, 0)
        go = jnp.zeros((), jnp.bool_)
        for row in (0, 1):
            top = jnp.max(bound[row] - st[row][0])
            go = go | (top - cend_ref[(2 * pair + row) * nkb + kb] >= EXP_UNDERFLOW - BOUND_SLACK)
        return go.astype(jnp.int32)

    one = (jnp.full((FOX_TQ, 1), NEG_BIG, F32), jnp.zeros((FOX_TQ, 1), F32), jnp.zeros((FOX_TQ, LANES), F32))
    st = (one, one)
    for j in reversed(range(ratio)):
        st = both(qi * ratio + j, rr - cc >= j * FOX_TK, st)

    def cond(c):
        return (c[0] >= 0) & (c[1] > 0)

    def body(c):
        st = both(c[0], None, c[2])
        return c[0] - 1, needed(c[0] - 1, st), st

    kb0 = qi * ratio - 1
    _, _, st = lax.while_loop(cond, body, (kb0, needed(kb0, st), st))
    (_, l_e, acc_e), (_, l_o, acc_o) = st
    o_ref[...] = jnp.where(low, acc_o / l_o, acc_e / l_e).astype(BF16)


def _forgetting(main, c, cend):
    t = main.shape[0]
    grid_spec = pltpu.PrefetchScalarGridSpec(
        num_scalar_prefetch=1,
        grid=(N_FOX_HEADS // 2, t // FOX_TQ),
        in_specs=[pl.BlockSpec((FOX_TQ, LANES), lambda p, i, ce: (i, FOX_Q_BLK0 + p)),
                  pl.BlockSpec((t, LANES), lambda p, i, ce: (0, FOX_KV_BLK0 + 2 * p)),
                  pl.BlockSpec((t, LANES), lambda p, i, ce: (0, FOX_KV_BLK0 + 2 * p + 1)),
                  pl.BlockSpec((None, 2, t), lambda p, i, ce: (p, 0, 0))],
        out_specs=pl.BlockSpec((FOX_TQ, LANES), lambda p, i, ce: (i, p)),
        scratch_shapes=[pltpu.VMEM((8, LANES), F32)],
    )
    return pl.pallas_call(
        _fox_kernel,
        grid_spec=grid_spec,
        out_shape=jax.ShapeDtypeStruct((t, N_FOX_HEADS * HEAD_DIM), BF16),
        compiler_params=_cparams(("parallel", "arbitrary")),
        name="forgetting_attention",
    )(cend, main, main, main, c)


def _layer_norm(x, g, b):
    mu = jnp.mean(x, axis=-1, keepdims=True)
    xc = x - mu
    var = jnp.mean(xc * xc, axis=-1, keepdims=True)
    return xc * lax.rsqrt(var + EPS) * g + b


def _outproj_kernel(yn_ref, ys_ref, yf_ref, h_ref, gn_ref, w_ref, g_ref, b_ref, o_ref):
    gn = gn_ref[...]
    parts = []
    lo = 0
    for y_ref in (yn_ref, ys_ref, yf_ref):
        yp = y_ref[...].astype(F32)
        hi = lo + yp.shape[1]
        ms = jnp.mean(yp * yp, axis=-1, keepdims=True)
        parts.append((yp * lax.rsqrt(ms + EPS) * gn[:, lo:hi]).astype(BF16))
        lo = hi
    mix = _dot(jnp.concatenate(parts, axis=-1), w_ref[...])
    o_ref[...] = _layer_norm(ALPHA * h_ref[...] + mix, g_ref[...], b_ref[...])


def _outproj(y_nsa, y_sb, y_fox, h, gn, w, g, b):
    t, d = h.shape
    tm = min(512, t)
    row = lambda i: (i, 0)
    full = lambda i: (0, 0)
    return pl.pallas_call(
        _outproj_kernel,
        grid=(t // tm,),
        in_specs=[pl.BlockSpec((tm, y_nsa.shape[1]), row), pl.BlockSpec((tm, y_sb.shape[1]), row),
                  pl.BlockSpec((tm, y_fox.shape[1]), row), pl.BlockSpec((tm, d), row), pl.BlockSpec((1, d), full),
                  pl.BlockSpec((d, d), full), pl.BlockSpec((1, d), full), pl.BlockSpec((1, d), full)],
        out_specs=pl.BlockSpec((tm, d), row),
        out_shape=jax.ShapeDtypeStruct((t, d), F32),
        compiler_params=_cparams(("parallel",)),
        name="norm_outproj_ln",
    )(y_nsa, y_sb, y_fox, h, gn, w, g, b)


def _moe_kernel(h_ref, rw_ref, rb_ref, wgu_ref, wd_ref, g_ref, b_ref, o_ref, xb_ref, gate_ref, acc_ref):
    e = pl.program_id(1)

    @pl.when(e == 0)
    def _():
        x = h_ref[...]
        xh, xm, _ = _split3(x)
        xb_ref[...] = xh
        w_hi = rw_ref[0]
        w_lo = rw_ref[1]
        logits = _dot(xh, w_hi) + _dot(xm, w_hi) + _dot(xh, w_lo) + rb_ref[...]
        lane = lax.broadcasted_iota(jnp.int32, logits.shape, 1)
        is_g = lane < N_GROUPS
        gl = jnp.where(is_g, logits, -jnp.inf)
        gmax = jnp.max(gl, axis=-1, keepdims=True)
        g_sel = jnp.min(jnp.where(gl == gmax, lane, LANES), axis=-1, keepdims=True)
        g_w = 1.0 / jnp.sum(jnp.where(is_g, jnp.exp(gl - gmax), 0.0), axis=-1, keepdims=True)
        ex = lane - N_GROUPS
        in_grp = (ex >= 0) & (ex < N_EXPERTS) & (ex // EXPERTS_PER_GROUP == g_sel)
        el = jnp.where(in_grp, logits, -jnp.inf)
        v1 = jnp.max(el, axis=-1, keepdims=True)
        i1 = jnp.min(jnp.where(el == v1, lane, LANES), axis=-1, keepdims=True)
        el2 = jnp.where(lane == i1, -jnp.inf, el)
        v2 = jnp.max(el2, axis=-1, keepdims=True)
        i2 = jnp.min(jnp.where(el2 == v2, lane, LANES), axis=-1, keepdims=True)
        e2 = jnp.exp(v2 - v1)
        w1 = g_w / (1.0 + e2)
        w2 = g_w * e2 / (1.0 + e2)
        gate_ref[...] = jnp.where(lane == i1, w1, 0.0) + jnp.where(lane == i2, w2, 0.0)
        acc_ref[...] = jnp.zeros_like(acc_ref)

    xb = xb_ref[...]
    gu = _dot(xb, wgu_ref[...])
    lane = lax.broadcasted_iota(jnp.int32, gate_ref.shape, 1)
    gcol = jnp.sum(jnp.where(lane == e + N_GROUPS, gate_ref[...], 0.0), axis=-1, keepdims=True)
    hid = jax.nn.silu(gu[:, :D_FF]) * gu[:, D_FF:]
    acc_ref[...] += _dot((hid * gcol).astype(BF16), wd_ref[...])

    @pl.when(e == N_EXPERTS - 1)
    def _():
        o_ref[...] = _layer_norm(ALPHA * h_ref[...] + acc_ref[...], g_ref[...], b_ref[...])


def _moe(h, rw, rb, wgu, wd, g, b):
    t, d = h.shape
    tm = min(512, t)
    return pl.pallas_call(
        _moe_kernel,
        grid=(t // tm, N_EXPERTS),
        in_specs=[pl.BlockSpec((tm, d), lambda i, e: (i, 0)),
                  pl.BlockSpec((2, d, LANES), lambda i, e: (0, 0, 0)),
                  pl.BlockSpec((1, LANES), lambda i, e: (0, 0)),
                  pl.BlockSpec((None, d, 2 * D_FF), lambda i, e: (e, 0, 0)),
                  pl.BlockSpec((None, D_FF, d), lambda i, e: (e, 0, 0)),
                  pl.BlockSpec((1, d), lambda i, e: (0, 0)),
                  pl.BlockSpec((1, d), lambda i, e: (0, 0))],
        out_specs=pl.BlockSpec((tm, d), lambda i, e: (i, 0)),
        out_shape=jax.ShapeDtypeStruct((t, d), F32),
        scratch_shapes=[pltpu.VMEM((tm, d), BF16), pltpu.VMEM((tm, LANES), F32), pltpu.VMEM((tm, d), F32)],
        compiler_params=_cparams(("parallel", "arbitrary")),
        name="moe_ln",
    )(h, rw, rb, wgu, wd, g, b)


def _cmp_to_slc_matrix(n_cmp_pad, n_slc):
    r, c = SLC_LEN // CMP_STRIDE, CMP_LEN // CMP_STRIDE
    m = np.zeros((n_cmp_pad, n_slc), np.float32)
    for j in range(n_slc):
        for a in range(r):
            for bb in range(c):
                i = r * j + a + bb
                if i < n_cmp_pad - 1:
                    m[i, j] += 1.0
    return m


def _compress_weights(pos, w1, w2):
    eye = jnp.eye(N_NSA_KV, dtype=F32)
    half = CMP_LEN // 2
    cols = N_NSA_KV * HEAD_DIM

    def expand(w):
        return jnp.einsum('lde,gh->lgdhe', w, eye).reshape(half * cols, cols).astype(BF16)

    def tile_pos(p):
        return jnp.broadcast_to(p[:, None, :], (half, N_NSA_KV, HEAD_DIM)).reshape(1, half * cols)

    pos2 = jnp.concatenate([tile_pos(pos[:half]), tile_pos(pos[half:])], axis=0)
    w2bd = jnp.einsum('ef,gh->gehf', w2, eye).reshape(cols, cols).astype(BF16)
    return pos2, expand(w1[:half]), expand(w1[half:]), w2bd


def _mixer(h, w_in, cmp_pos_k, cmp_w1_k, cmp_w2_k, cmp_pos_v, cmp_w1_v, cmp_w2_v, fox_forget_bias,
           norm_nsa, norm_sb, norm_fox, w_out, ln_g, ln_b):
    t = h.shape[0]
    nq = t // NSA_QB
    w_main = w_in[:, _PERM_MAIN].astype(BF16)
    w_tail = jnp.pad(w_in[:, _PERM_TAIL], ((0, 0), (0, LANES - _PERM_TAIL.size))).astype(BF16)
    main, tail = _inproj(h, w_main, w_tail)

    n_rows = t // CMP_STRIDE
    kc = _compress(main[:, 1024:1280].reshape(n_rows, -1), *_compress_weights(cmp_pos_k, cmp_w1_k, cmp_w2_k))
    vc = _compress(main[:, 1280:1536].reshape(n_rows, -1), *_compress_weights(cmp_pos_v, cmp_w1_v, cmp_w2_v))
    kvc = jnp.concatenate([kc.reshape(n_rows, N_NSA_KV, HEAD_DIM), vc.reshape(n_rows, N_NSA_KV, HEAD_DIM)],
                          axis=-1).transpose(1, 0, 2)
    slopes = 2.0 **(-8.0 * jnp.arange(1, N_NSA_HEADS + 1, dtype=F32) / N_NSA_HEADS)
    slopes = jnp.repeat(slopes.reshape(N_NSA_KV, NSA_HPG), NSA_QB, axis=1)[..., None]
    imap = jnp.asarray(_cmp_to_slc_matrix(n_rows, t // SLC_LEN), BF16)
    oc, sel, flags = _nsa_cmp(main, kvc, imap, slopes)
    n_chunks = t // SLC_CHUNK
    diag = (jnp.arange(nq) * NSA_QB) // SLC_CHUNK
    active = flags[:, :, 0, :n_chunks] * (jnp.arange(n_chunks)[None, None, :] != diag[None, :, None])
    idx = jnp.argsort(1 - active, axis=-1, stable=True).astype(jnp.int32).reshape(-1)
    cnt = jnp.sum(active, axis=-1).astype(jnp.int32).reshape(-1)
    gates = tail[:, :48].reshape(nq, NSA_QB, N_NSA_KV, NSA_HPG, 3).transpose(2, 0, 3, 1, 4)
    gates = gates.reshape(N_NSA_KV, nq, NSA_HPG * NSA_QB, 3)
    o_nsa = _nsa_main(idx, cnt, main, sel, oc, gates, slopes)

    o_sb = _stick_breaking(main)

    nb = t // LANES
    ff_rows = tail[:, 48:56].T.reshape(N_FOX_HEADS * nb, LANES)
    bias_rows = jnp.repeat(fox_forget_bias, nb)[:, None]
    c = _forget_cumsum(ff_rows, bias_rows, nb).reshape(N_FOX_HEADS, t)
    cend = c[:, FOX_TK - 1::FOX_TK].reshape(-1)
    o_fox = _forgetting(main, c.reshape(N_FOX_HEADS // 2, 2, t), cend)

    gn = jnp.concatenate([norm_nsa, norm_sb, norm_fox])[_PERM_MIX][None, :]
    return _outproj(o_nsa, o_sb, o_fox, h, gn, w_out[_PERM_MIX].astype(BF16), ln_g[None, :], ln_b[None, :])


def _ffn(h, rg_w, rg_b, re_w, re_b, w_gate, w_up, w_down, ln_g, ln_b):
    rw = jnp.pad(jnp.concatenate([rg_w, re_w], axis=1), ((0, 0), (0, LANES - N_GROUPS - N_EXPERTS)))
    rw_hi = rw.astype(BF16)
    rw_lo = (rw - rw_hi.astype(F32)).astype(BF16)
    rb = jnp.pad(jnp.concatenate([rg_b, re_b]), (0, LANES - N_GROUPS - N_EXPERTS))[None, :]
    wgu = jnp.concatenate([w_gate, w_up], axis=-1).astype(BF16)
    return _moe(h, jnp.stack([rw_hi, rw_lo]), rb, wgu, w_down.astype(BF16), ln_g[None, :], ln_b[None, :])


def kernel(x, w_in, cmp_pos_k, cmp_w1_k, cmp_w2_k, cmp_pos_v, cmp_w1_v, cmp_w2_v, fox_forget_bias, norm_nsa, norm_sb, norm_fox, w_out, ln1_g, ln1_b, router_group_w, router_group_b, router_expert_w, router_expert_b, expert_w_gate, expert_w_up, expert_w_down, ln2_g, ln2_b):
    b, t, d = x.shape
    assert b == 1 and d == D_MODEL and t % 1024 == 0
    h = x.reshape(t, d)
    for l in range(DEPTH):
        h = _mixer(h, w_in[l], cmp_pos_k[l], cmp_w1_k[l], cmp_w2_k[l], cmp_pos_v[l], cmp_w1_v[l], cmp_w2_v[l],
                   fox_forget_bias[l], norm_nsa[l], norm_sb[l], norm_fox[l], w_out[l], ln1_g[l], ln1_b[l])
        h = _ffn(h, router_group_w[l], router_group_b[l], router_expert_w[l], router_expert_b[l],
                 expert_w_gate[l], expert_w_up[l], expert_w_down[l], ln2_g[l], ln2_b[l])
    return h.reshape(b, t, d)
```

```python
import functools

import numpy as np
import jax
import jax.numpy as jnp
from jax import lax
from jax.experimental import pallas as pl
from jax.experimental.pallas import tpu as pltpu

F32 = jnp.float32
BF16 = jnp.bfloat16

D_MODEL = 2048
DEPTH = 2
HEAD_DIM = 64
N_NSA_HEADS = 16
N_NSA_KV = 4
NSA_HPG = 4
N_SB_HEADS = 8
N_FOX_HEADS = 8
CMP_LEN = 32
CMP_STRIDE = 16
SLC_LEN = 64
SLC_TOPK = 8
WINDOW = 512
N_GROUPS = 4
EXPERTS_PER_GROUP = 4
N_EXPERTS = 16
D_FF = D_MODEL // 8
ALPHA = (2 * DEPTH) ** 0.25
EPS = 1e-5
NEG_BIG = -1e30
SEL_BONUS = 1e6
QK_SCALE = HEAD_DIM ** -0.5

LANES = 128
NSA_QB = 128
SLC_CHUNK = 256
BLK_PER_CHUNK = SLC_CHUNK // SLC_LEN
SB_BLK = 256
FOX_TQ = 512
FOX_TK = 512
EXP_UNDERFLOW = -104.0
BOUND_SLACK = 6.0
SLC_BLK0, WIN_BLK0 = 1536 // 128, 2048 // 128
SB_Q_BLK0, SB_KV_BLK0 = 2560 // 128, 3072 // 128
FOX_Q_BLK0, FOX_KV_BLK0 = 4096 // 128, 4608 // 128
MAIN_W = 5632
VMEM_LIMIT = 56 * 1024 * 1024

_PROJ_SIZES = (1024, 256, 256, 256, 256, 256, 256, 48, 512, 512, 512, 512, 512, 512, 8)
_OFF = np.concatenate([[0], np.cumsum(_PROJ_SIZES)])


def _perm_columns():
    seg = lambda i: np.arange(_OFF[i], _OFF[i + 1])
    nq, ck, cv, sk, sv, wk, wv, ng, sbq, sbk, sbv, fq, fk, fv, ff = [seg(i) for i in range(15)]

    def interleave(k, v, n, swap_odd=False):
        out = []
        for h in range(n):
            kh, vh = k[h * 64:(h + 1) * 64], v[h * 64:(h + 1) * 64]
            out.append(np.concatenate([vh, kh] if (swap_odd and h % 2) else [kh, vh]))
        return np.concatenate(out)

    main = np.concatenate([nq, ck, cv, interleave(sk, sv, 4), interleave(wk, wv, 4), sbq,
                           interleave(sbk, sbv, 8, True), fq, interleave(fk, fv, 8, True)])
    tail = np.concatenate([ng, ff])
    assert main.size == MAIN_W
    return main, tail


_PERM_MAIN, _PERM_TAIL = _perm_columns()


def _perm_mixer_columns():
    def pairs(base, n_heads):
        order = [h for p in range(n_heads // 2) for h in (2 * p + 1, 2 * p)]
        return np.concatenate([base + h * HEAD_DIM + np.arange(HEAD_DIM) for h in order])

    nsa_w = N_NSA_HEADS * HEAD_DIM
    sb_w = N_SB_HEADS * HEAD_DIM
    return np.concatenate([pairs(0, N_NSA_HEADS), pairs(nsa_w, N_SB_HEADS), pairs(nsa_w + sb_w, N_FOX_HEADS)])


_PERM_MIX = _perm_mixer_columns()


def _cparams(sem):
    return pltpu.CompilerParams(dimension_semantics=sem, vmem_limit_bytes=VMEM_LIMIT)


def _dot(a, b):
    return jnp.dot(a, b, preferred_element_type=F32)


def _dot_nt(a, b):
    return lax.dot_general(a, b, (((1,), (1,)), ((), ())), preferred_element_type=F32)


def _split3(x):
    hi = x.astype(BF16)
    r = x - hi.astype(F32)
    mid = r.astype(BF16)
    lo = (r - mid.astype(F32)).astype(BF16)
    return hi, mid, lo


def _dot_split(x, w, parts):
    pieces = _split3(x)[:parts]
    out = _dot(pieces[0], w)
    for p in pieces[1:]:
        out = out + _dot(p, w)
    return out


def _softplus(z):
    return jnp.maximum(z, 0.0) + jnp.log1p(jnp.exp(-jnp.abs(z)))


def _inproj_kernel(x_ref, w_ref, wt_ref, o_ref, ot_ref, xb_ref):
    @pl.when(pl.program_id(1) == 0)
    def _():
        xb = x_ref[...].astype(BF16)
        xb_ref[...] = xb
        ot_ref[...] = _dot(xb, wt_ref[...])

    o_ref[...] = _dot(xb_ref[...], w_ref[...]).astype(BF16)


def _inproj(h, w_main, w_tail):
    t, d = h.shape
    tm, tn = min(1024, t), 512
    return pl.pallas_call(
        _inproj_kernel,
        grid=(t // tm, MAIN_W // tn),
        in_specs=[pl.BlockSpec((tm, d), lambda i, j: (i, 0)),
                  pl.BlockSpec((d, tn), lambda i, j: (0, j)),
                  pl.BlockSpec((d, LANES), lambda i, j: (0, 0))],
        out_specs=[pl.BlockSpec((tm, tn), lambda i, j: (i, j)),
                   pl.BlockSpec((tm, LANES), lambda i, j: (i, 0))],
        out_shape=[jax.ShapeDtypeStruct((t, MAIN_W), BF16), jax.ShapeDtypeStruct((t, LANES), F32)],
        scratch_shapes=[pltpu.VMEM((tm, d), BF16)],
        compiler_params=_cparams(("parallel", "arbitrary")),
        name="inproj",
    )(h, w_main, w_tail)


def _cumsum_kernel(ff_ref, b_ref, c_ref, *, nb):
    x = ff_ref[...] + b_ref[...]
    lf = -_softplus(-x)
    r = lf.shape[0]
    jj = lax.broadcasted_iota(jnp.int32, (LANES, LANES), 0)
    kk = lax.broadcasted_iota(jnp.int32, (LANES, LANES), 1)
    upper = jnp.where(jj <= kk, 1.0, 0.0).astype(BF16)
    ones = jnp.ones((LANES, LANES), BF16)
    within = _dot_split(lf, upper, 3)
    tot = _dot_split(lf, ones, 3)
    rr = lax.broadcasted_iota(jnp.int32, (r, r), 0)
    cc = lax.broadcasted_iota(jnp.int32, (r, r), 1)
    earlier = jnp.where((cc < rr) & (cc // nb == rr // nb), 1.0, 0.0).astype(BF16)
    hi, mid, lo = _split3(tot)
    off = _dot(earlier, hi) + _dot(earlier, mid) + _dot(earlier, lo)
    c_ref[...] = within + off


def _forget_cumsum(ff_rows, bias_rows, nb):
    r = ff_rows.shape[0]
    return pl.pallas_call(
        functools.partial(_cumsum_kernel, nb=nb),
        out_shape=jax.ShapeDtypeStruct((r, LANES), F32),
        compiler_params=pltpu.CompilerParams(vmem_limit_bytes=VMEM_LIMIT),
        name="forget_cumsum",
    )(ff_rows, bias_rows)


def _gelu_tanh(x):
    return 0.5 * x * (1.0 + jnp.tanh(np.sqrt(2.0 / np.pi).astype(np.float32) * (x + 0.044715 * (x * x * x))))


def _compress_kernel(x_ref, pos_ref, wa_ref, wb_ref, w2_ref, o_ref):
    x = x_ref[...].astype(F32)
    n = x.shape[0]
    first = _dot((x + pos_ref[0:1, :]).astype(BF16), wa_ref[...])
    second = _dot((x + pos_ref[1:2, :]).astype(BF16), wb_ref[...])
    pre = first + pltpu.roll(second, n - 1, 0)
    o_ref[...] = _dot(_gelu_tanh(pre).astype(BF16), w2_ref[...]).astype(BF16)


def _compress(x2, pos2, wa, wb, w2):
    n = x2.shape[0]
    return pl.pallas_call(
        _compress_kernel,
        out_shape=jax.ShapeDtypeStruct((n, N_NSA_KV * HEAD_DIM), BF16),
        compiler_params=pltpu.CompilerParams(vmem_limit_bytes=VMEM_LIMIT),
        name="nsa_compress",
    )(x2, pos2, wa, wb, w2)


def _group_queries(q_ref):
    q = q_ref[...]
    low = lax.broadcasted_iota(jnp.int32, (NSA_QB, LANES), 1) < HEAD_DIM
    zero = jnp.zeros((NSA_QB, LANES), BF16)
    out = []
    for j in range(NSA_HPG // 2):
        pair = q[:, j * LANES:(j + 1) * LANES]
        swapped = pltpu.roll(pair.astype(F32), HEAD_DIM, 1).astype(BF16)
        out += [jnp.where(low, pair, zero), jnp.where(low, swapped, zero)]
    return jnp.concatenate(out, axis=0) * QK_SCALE


def _head_sum(p):
    return p[0:NSA_QB] + p[NSA_QB:2 * NSA_QB] + p[2 * NSA_QB:3 * NSA_QB] + p[3 * NSA_QB:4 * NSA_QB]


def _nsa_cmp_kernel(q_ref, kvc_ref, imap_ref, slope_ref, oc_ref, sel_ref, flag_ref, s_ref, *, chunk):
    t0 = pl.program_id(1) * NSA_QB
    rows = NSA_HPG * NSA_QB
    q = _group_queries(q_ref)
    slope = slope_ref[...]
    n_slc = sel_ref.shape[-1]
    trow = lax.broadcasted_iota(jnp.int32, (rows, 1), 0) % NSA_QB
    n_valid = t0 // CMP_STRIDE + (NSA_QB - CMP_LEN) // CMP_STRIDE + 1
    n_steps = (n_valid + chunk - 1) // chunk
    col = lax.broadcasted_iota(jnp.int32, (1, chunk), 1)

    def logits(c, m):
        n0 = pl.multiple_of(c * chunk, chunk)
        end_rel = (n0 + col) * CMP_STRIDE + (CMP_LEN - 1) - t0
        s = _dot_nt(q, kvc_ref[pl.ds(n0, chunk), :]) + slope * end_rel.astype(F32)
        s = jnp.where(end_rel <= trow, s, NEG_BIG)
        s_ref[:, pl.ds(n0, chunk)] = s
        return jnp.maximum(m, jnp.max(s, axis=-1, keepdims=True))

    m = lax.fori_loop(0, n_steps, logits, jnp.full((rows, 1), NEG_BIG, F32))
    m = jnp.where(m > 0.5 * NEG_BIG, m, 0.0)
    lane = lax.broadcasted_iota(jnp.int32, (chunk, LANES), 1)

    def weights(c, carry):
        acc, imp = carry
        n0 = pl.multiple_of(c * chunk, chunk)
        e = jnp.exp(s_ref[:, pl.ds(n0, chunk)] - m).astype(BF16)
        kvc = kvc_ref[pl.ds(n0, chunk), :]
        acc = acc + _dot(e, jnp.where(lane == 0, jnp.ones_like(kvc), kvc))
        return acc, imp + _dot(e, imap_ref[pl.ds(n0, chunk), :])

    acc, imp = lax.fori_loop(0, n_steps, weights,
                             (jnp.zeros((rows, LANES), F32), jnp.zeros((rows, n_slc), F32)))
    l = acc[:, 0:1]
    inv = 1.0 / jnp.where(l > 0.0, l, 1.0)
    oc_ref[...] = (acc * inv).reshape(NSA_HPG, NSA_QB, LANES)
    imp = _head_sum(imp * inv)

    tq = t0 + lax.broadcasted_iota(jnp.int32, (1, NSA_QB), 1)
    jb = lax.broadcasted_iota(jnp.int32, (n_slc, 1), 0)
    jbf = jb.astype(F32)
    cur = tq // SLC_LEN
    forced = (jb == 0) | (jb == cur) | (jb == cur - 1)
    valid = jb * SLC_LEN <= tq
    score = jnp.where(valid, imp.T + jnp.where(forced, SEL_BONUS, 0.0), -jnp.inf)
    sel = jnp.zeros((n_slc, NSA_QB), F32)
    for _ in range(SLC_TOPK):
        best = jnp.max(score, axis=0, keepdims=True)
        first = jnp.min(jnp.where(score == best, jbf, float(n_slc)), axis=0, keepdims=True)
        hit = jbf == first
        sel = jnp.where(hit, 1.0, sel)
        score = jnp.where(hit, -jnp.inf, score)
    sel = jnp.where(valid, sel, 0.0).T
    sel_ref[...] = sel.astype(BF16)
    any_q = jnp.max(sel, axis=0, keepdims=True)
    jr = lax.broadcasted_iota(jnp.int32, (n_slc, LANES), 0)
    cr = lax.broadcasted_iota(jnp.int32, (n_slc, LANES), 1)
    group = jnp.where(jr // BLK_PER_CHUNK == cr, 1.0, 0.0).astype(BF16)
    per_chunk = _dot(jnp.broadcast_to(any_q, (8, n_slc)).astype(BF16), group)
    flag_ref[...] = jnp.where(per_chunk > 0.5, 1, 0).astype(jnp.int32)


def _nsa_cmp(main, kvc, imap, slopes):
    t = main.shape[0]
    nq = t // NSA_QB
    n = kvc.shape[1]
    n_slc = t // SLC_LEN
    rows = NSA_HPG * NSA_QB
    chunk = min(256, n)
    return pl.pallas_call(
        functools.partial(_nsa_cmp_kernel, chunk=chunk),
        grid=(N_NSA_KV, nq),
        in_specs=[pl.BlockSpec((NSA_QB, NSA_HPG * HEAD_DIM), lambda g, i: (i, g)),
                  pl.BlockSpec((None, n, LANES), lambda g, i: (g, 0, 0)),
                  pl.BlockSpec((n, n_slc), lambda g, i: (0, 0)),
                  pl.BlockSpec((None, rows, 1), lambda g, i: (g, 0, 0))],
        out_specs=[pl.BlockSpec((NSA_HPG, NSA_QB, LANES), lambda g, i: (g, i, 0)),
                   pl.BlockSpec((None, NSA_QB, n_slc), lambda g, i: (g, i, 0)),
                   pl.BlockSpec((None, None, 8, LANES), lambda g, i: (g, i, 0, 0))],
        out_shape=[jax.ShapeDtypeStruct((N_NSA_HEADS, t, LANES), F32),
                   jax.ShapeDtypeStruct((N_NSA_KV, t, n_slc), BF16),
                   jax.ShapeDtypeStruct((N_NSA_KV, nq, 8, LANES), jnp.int32)],
        scratch_shapes=[pltpu.VMEM((rows, n), F32)],
        compiler_params=_cparams(("parallel", "arbitrary")),
        name="nsa_cmp_select",
    )(main, kvc, imap, slopes)


def _nsa_main_kernel(idx_ref, cnt_ref, q_ref, slc_ref, win_ref, sel_ref, oc_ref, gate_ref, slope_ref, o_ref,
                     *, n_chunks):
    g = pl.program_id(0)
    qb = pl.program_id(1)
    nq = pl.num_programs(1)
    t0 = qb * NSA_QB
    rows = NSA_HPG * NSA_QB
    q = _group_queries(q_ref)
    slope = slope_ref[...]
    trow = lax.broadcasted_iota(jnp.int32, (rows, 1), 0) % NSA_QB
    n_slc = sel_ref.shape[-1]
    sel = sel_ref[...]

    def with_ones(kv):
        lane = lax.broadcasted_iota(jnp.int32, kv.shape, 1)
        return jnp.where(lane == 0, jnp.ones_like(kv), kv)

    band = WINDOW + NSA_QB
    start = pl.multiple_of(jnp.maximum(t0 - WINDOW, 0), NSA_QB)
    kw = win_ref[pl.ds(start, band), :]
    krel = (start - t0) + lax.broadcasted_iota(jnp.int32, (1, band), 1)
    dist = trow - krel
    ok = (dist >= 0) & (dist < WINDOW)
    s = jnp.where(ok, _dot_nt(q, kw) + slope * krel.astype(F32), NEG_BIG)
    m = jnp.max(s, axis=-1, keepdims=True)
    e = jnp.where(ok, jnp.exp(s - m), 0.0)
    o_w = _dot(e.astype(BF16), with_ones(kw))
    o_w = o_w / o_w[:, 0:1]

    def picked_keys(block_of_key):
        jr = lax.broadcasted_iota(jnp.int32, (n_slc, block_of_key.shape[1]), 0)
        expand = jnp.where(jr == block_of_key, 1.0, 0.0).astype(BF16)
        return jnp.concatenate([_dot(sel, expand)] * NSA_HPG, axis=0)

    cd = t0 // SLC_CHUNK
    k0 = pl.multiple_of(cd * SLC_CHUNK, SLC_CHUNK)
    kv = slc_ref[pl.ds(k0, SLC_CHUNK), :]
    cr = lax.broadcasted_iota(jnp.int32, (1, SLC_CHUNK), 1)
    krel = (k0 - t0) + cr
    okc = (picked_keys(cd * BLK_PER_CHUNK + cr // SLC_LEN) > 0.5) & (krel <= trow)
    s = jnp.where(okc, _dot_nt(q, kv) + slope * krel.astype(F32), NEG_BIG)
    m = jnp.max(s, axis=-1, keepdims=True)
    p = jnp.where(okc, jnp.exp(s - m), 0.0)
    acc = _dot(p.astype(BF16), with_ones(kv))

    base = (g * nq + qb) * n_chunks
    cnt = cnt_ref[g * nq + qb]
    cr2 = lax.broadcasted_iota(jnp.int32, (1, 2 * SLC_CHUNK), 1)
    second = cr2 >= SLC_CHUNK

    def two_chunks(i, carry):
        m, acc = carry
        c1 = idx_ref[base + 2 * i]
        has2 = 2 * i + 1 < cnt
        c2 = jnp.where(has2, idx_ref[base + 2 * i + 1], c1)
        kv = jnp.concatenate([slc_ref[pl.ds(pl.multiple_of(c1 * SLC_CHUNK, SLC_CHUNK), SLC_CHUNK), :],
                              slc_ref[pl.ds(pl.multiple_of(c2 * SLC_CHUNK, SLC_CHUNK), SLC_CHUNK), :]], axis=0)
        blk2 = jnp.where(has2, (c2 - 1) * BLK_PER_CHUNK, -4 * n_slc)
        block_of_key = jnp.where(second, blk2, c1 * BLK_PER_CHUNK) + cr2 // SLC_LEN
        krel = (jnp.where(second, (c2 - 1) * SLC_CHUNK, c1 * SLC_CHUNK) - t0 + cr2).astype(F32)
        s = _dot_nt(q, kv) + slope * krel + (picked_keys(block_of_key) - 1.0) * (-NEG_BIG)
        m_new = jnp.maximum(m, jnp.max(s, axis=-1, keepdims=True))
        p = jnp.exp(s - m_new)
        return m_new, jnp.exp(m - m_new) * acc + _dot(p.astype(BF16), with_ones(kv))

    m, acc = lax.fori_loop(0, (cnt + 1) // 2, two_chunks, (m, acc))
    o_s = acc / acc[:, 0:1]

    gt = jax.nn.sigmoid(gate_ref[...])
    o_c = oc_ref[...].reshape(rows, LANES)
    o = gt[:, 0:1] * o_c + gt[:, 1:2] * o_s + gt[:, 2:3] * o_w
    low = lax.broadcasted_iota(jnp.int32, (NSA_QB, LANES), 1) < HEAD_DIM
    pairs = []
    for j in range(NSA_HPG // 2):
        even = o[2 * j * NSA_QB:(2 * j + 1) * NSA_QB]
        odd = o[(2 * j + 1) * NSA_QB:(2 * j + 2) * NSA_QB]
        pairs.append(jnp.where(low, pltpu.roll(odd, HEAD_DIM, 1), even))
    o_ref[...] = jnp.concatenate(pairs, axis=1).astype(BF16)


def _nsa_main(idx, cnt, main, sel, oc, gates, slopes):
    t = main.shape[0]
    nq = t // NSA_QB
    n_slc = t // SLC_LEN
    n_chunks = t // SLC_CHUNK
    rows = NSA_HPG * NSA_QB
    width = NSA_HPG * HEAD_DIM
    grid_spec = pltpu.PrefetchScalarGridSpec(
        num_scalar_prefetch=2,
        grid=(N_NSA_KV, nq),
        in_specs=[pl.BlockSpec((NSA_QB, width), lambda g, i, ix, ct: (i, g)),
                  pl.BlockSpec((t, LANES), lambda g, i, ix, ct: (0, SLC_BLK0 + g)),
                  pl.BlockSpec((t, LANES), lambda g, i, ix, ct: (0, WIN_BLK0 + g)),
                  pl.BlockSpec((None, NSA_QB, n_slc), lambda g, i, ix, ct: (g, i, 0)),
                  pl.BlockSpec((NSA_HPG, NSA_QB, LANES), lambda g, i, ix, ct: (g, i, 0)),
                  pl.BlockSpec((None, None, rows, 3), lambda g, i, ix, ct: (g, i, 0, 0)),
                  pl.BlockSpec((None, rows, 1), lambda g, i, ix, ct: (g, 0, 0))],
        out_specs=pl.BlockSpec((NSA_QB, width), lambda g, i, ix, ct: (i, g)),
    )
    return pl.pallas_call(
        functools.partial(_nsa_main_kernel, n_chunks=n_chunks),
        grid_spec=grid_spec,
        out_shape=jax.ShapeDtypeStruct((t, N_NSA_HEADS * HEAD_DIM), BF16),
        compiler_params=_cparams(("parallel", "arbitrary")),
        name="nsa_select_window",
    )(idx, cnt, main, main, main, sel, oc, gates, slopes)


def _pair_queries(q_ref):
    q = q_ref[...]
    low = lax.broadcasted_iota(jnp.int32, q.shape, 1) < HEAD_DIM
    zero = jnp.zeros_like(q)
    return jnp.where(low, q, zero) * QK_SCALE, jnp.where(low, zero, q) * QK_SCALE, low


def _sb_kernel(q_ref, kve_ref, kvo_ref, o_ref):
    qi = pl.program_id(1)
    q_e, q_o, low = _pair_queries(q_ref)
    rr = lax.broadcasted_iota(jnp.int32, (SB_BLK, SB_BLK), 0)
    cc = lax.broadcasted_iota(jnp.int32, (SB_BLK, SB_BLK), 1)
    later = jnp.where(rr > cc, 1.0, 0.0).astype(BF16)
    ones = jnp.ones((SB_BLK, LANES), BF16)

    def head_block(q, kv_ref, kb, mask, run, acc):
        kv = kv_ref[pl.ds(pl.multiple_of(kb * SB_BLK, SB_BLK), SB_BLK), :]
        z = _dot_nt(q, kv)
        sp = _softplus(z)
        log_1mb = -sp if mask is None else jnp.where(mask, -sp, 0.0)
        hi, lo = _split3(log_1mb)[:2]
        within = _dot(hi, later) + _dot(lo, later)
        total = _dot(hi, ones) + _dot(lo, ones)
        a = jnp.exp(z - sp + within + run[:, 0:1])
        if mask is not None:
            a = jnp.where(mask, a, 0.0)
        return run + total, acc + _dot(a.astype(BF16), kv)

    def both(kb, mask, st):
        run_e, acc_e, run_o, acc_o = st
        run_e, acc_e = head_block(q_e, kve_ref, kb, mask, run_e, acc_e)
        run_o, acc_o = head_block(q_o, kvo_ref, kb, mask, run_o, acc_o)
        return run_e, acc_e, run_o, acc_o

    def alive(st):
        return (jnp.max(jnp.maximum(st[0], st[2])) > EXP_UNDERFLOW).astype(jnp.int32)

    zero = jnp.zeros((SB_BLK, LANES), F32)
    st = both(qi, cc < rr, (zero, zero, zero, zero))

    def cond(c):
        return (c[0] >= 0) & (c[1] > 0)

    def body(c):
        st = both(c[0], None, c[2])
        return c[0] - 1, alive(st), st

    _, _, st = lax.while_loop(cond, body, (qi - 1, alive(st), st))
    o_ref[...] = jnp.where(low, st[3], st[1]).astype(BF16)


def _stick_breaking(main):
    t = main.shape[0]
    return pl.pallas_call(
        _sb_kernel,
        grid=(N_SB_HEADS // 2, t // SB_BLK),
        in_specs=[pl.BlockSpec((SB_BLK, LANES), lambda p, i: (i, SB_Q_BLK0 + p)),
                  pl.BlockSpec((t, LANES), lambda p, i: (0, SB_KV_BLK0 + 2 * p)),
                  pl.BlockSpec((t, LANES), lambda p, i: (0, SB_KV_BLK0 + 2 * p + 1))],
        out_specs=pl.BlockSpec((SB_BLK, LANES), lambda p, i: (i, p)),
        out_shape=jax.ShapeDtypeStruct((t, N_SB_HEADS * HEAD_DIM), BF16),
        compiler_params=_cparams(("parallel", "arbitrary")),
        name="stick_breaking",
    )(main, main, main)


def _fox_kernel(cend_ref, q_ref, kve_ref, kvo_ref, c_ref, o_ref, kmax_ref):
    pair = pl.program_id(0)
    qi = pl.program_id(1)
    ratio = FOX_TQ // FOX_TK
    nkb = pl.num_programs(1) * ratio
    t = kve_ref.shape[0]
    scan = 1024

    @pl.when(qi == 0)
    def _():
        def max_norm(ref, keys_low):
            keep = (lax.broadcasted_iota(jnp.int32, (scan, LANES), 1) < HEAD_DIM) == keys_low

            def step(r, best):
                k = ref[pl.ds(pl.multiple_of(r * scan, scan), scan), :].astype(F32)
                k = jnp.where(keep, k, 0.0)
                return jnp.maximum(best, jnp.sum(k * k, axis=-1, keepdims=True))

            best = lax.fori_loop(0, t // scan, step, jnp.zeros((scan, 1), F32))
            return jnp.sqrt(jnp.max(best, axis=0, keepdims=True))

        kmax_ref[0:1, :] = jnp.broadcast_to(max_norm(kve_ref, True), (1, LANES))
        kmax_ref[1:2, :] = jnp.broadcast_to(max_norm(kvo_ref, False), (1, LANES))

    q_e, q_o, low = _pair_queries(q_ref)

    def logit_bound(q, row):
        qf = q.astype(F32)
        return jnp.sqrt(jnp.sum(qf * qf, axis=-1, keepdims=True)) * kmax_ref[row:row + 1, 0:1]

    bound = (logit_bound(q_e, 0), logit_bound(q_o, 1))
    rr = lax.broadcasted_iota(jnp.int32, (FOX_TQ, FOX_TK), 0)
    cc = lax.broadcasted_iota(jnp.int32, (FOX_TQ, FOX_TK), 1)

    key_lane = lax.broadcasted_iota(jnp.int32, (FOX_TK, LANES), 1)
    one_bf = jnp.ones((FOX_TK, LANES), BF16)

    def block_keys(kv_ref, kb):
        return kv_ref[pl.ds(pl.multiple_of(kb * FOX_TK, FOX_TK), FOX_TK), :]

    def scores(q, kv_ref, row, kb):
        k0 = pl.multiple_of(kb * FOX_TK, FOX_TK)
        return _dot_nt(q, block_keys(kv_ref, kb)) - c_ref[row:row + 1, pl.ds(k0, FOX_TK)]

    def consume(s, kv_ref, row, kb, mask, st):
        m, acc = st
        if mask is not None:
            s = jnp.where(mask, s, NEG_BIG)
        m_new = jnp.maximum(m, jnp.max(s, axis=-1, keepdims=True))
        p = jnp.exp(s - m_new)
        if mask is not None:
            p = jnp.where(mask, p, 0.0)
        vals = jnp.where(key_lane == row * HEAD_DIM, one_bf, block_keys(kv_ref, kb))
        return m_new, jnp.exp(m - m_new) * acc + _dot(p.astype(BF16), vals)

    def both_scores(kb):
        return scores(q_e, kve_ref, 0, kb), scores(q_o, kvo_ref, 1, kb)

    def both(s, kb, mask, st):
        return (consume(s[0], kve_ref, 0, kb, mask, st[0]), consume(s[1], kvo_ref, 1, kb, mask, st[1]))

    def needed(kb, st):
        kb = jnp.maximum(kb, 0)
        go = jnp.zeros((), jnp.bool_)
        for row in (0, 1):
            top = jnp.max(bound[row] - st[row][0])
            go = go | (top - cend_ref[(2 * pair + row) * nkb + kb] >= EXP_UNDERFLOW - BOUND_SLACK)
        return go.astype(jnp.int32)

    one = (jnp.full((FOX_TQ, 1), NEG_BIG, F32), jnp.zeros((FOX_TQ, LANES), F32))
    st = (one, one)
    for j in reversed(range(ratio)):
        kb = qi * ratio + j
        st = both(both_scores(kb), kb, rr - cc >= j * FOX_TK, st)

    def cond(c):
        return (c[0] >= 0) & (c[1] > 0)

    def body(c):
        st = both(both_scores(c[0]), c[0], None, c[2])
        return c[0] - 1, needed(c[0] - 1, st), st

    kb0 = qi * ratio - 1
    _, _, st = lax.while_loop(cond, body, (kb0, needed(kb0, st), st))
    (_, acc_e), (_, acc_o) = st
    o_e = acc_e / acc_e[:, 0:1]
    o_o = acc_o / acc_o[:, HEAD_DIM:HEAD_DIM + 1]
    o_ref[...] = jnp.where(low, o_o, o_e).astype(BF16)


def _forgetting(main, c, cend):
    t = main.shape[0]
    grid_spec = pltpu.PrefetchScalarGridSpec(
        num_scalar_prefetch=1,
        grid=(N_FOX_HEADS // 2, t // FOX_TQ),
        in_specs=[pl.BlockSpec((FOX_TQ, LANES), lambda p, i, ce: (i, FOX_Q_BLK0 + p)),
                  pl.BlockSpec((t, LANES), lambda p, i, ce: (0, FOX_KV_BLK0 + 2 * p)),
                  pl.BlockSpec((t, LANES), lambda p, i, ce: (0, FOX_KV_BLK0 + 2 * p + 1)),
                  pl.BlockSpec((None, 2, t), lambda p, i, ce: (p, 0, 0))],
        out_specs=pl.BlockSpec((FOX_TQ, LANES), lambda p, i, ce: (i, p)),
        scratch_shapes=[pltpu.VMEM((8, LANES), F32)],
    )
    return pl.pallas_call(
        _fox_kernel,
        grid_spec=grid_spec,
        out_shape=jax.ShapeDtypeStruct((t, N_FOX_HEADS * HEAD_DIM), BF16),
        compiler_params=_cparams(("parallel", "arbitrary")),
        name="forgetting_attention",
    )(cend, main, main, main, c)


def _layer_norm(x, g, b):
    mu = jnp.mean(x, axis=-1, keepdims=True)
    xc = x - mu
    var = jnp.mean(xc * xc, axis=-1, keepdims=True)
    return xc * lax.rsqrt(var + EPS) * g + b


def _outproj_kernel(yn_ref, ys_ref, yf_ref, h_ref, gn_ref, w_ref, g_ref, b_ref, o_ref):
    gn = gn_ref[...]
    parts = []
    lo = 0
    for y_ref in (yn_ref, ys_ref, yf_ref):
        yp = y_ref[...].astype(F32)
        hi = lo + yp.shape[1]
        ms = jnp.mean(yp * yp, axis=-1, keepdims=True)
        parts.append((yp * lax.rsqrt(ms + EPS) * gn[:, lo:hi]).astype(BF16))
        lo = hi
    mix = _dot(jnp.concatenate(parts, axis=-1), w_ref[...])
    o_ref[...] = _layer_norm(ALPHA * h_ref[...] + mix, g_ref[...], b_ref[...])


def _outproj(y_nsa, y_sb, y_fox, h, gn, w, g, b):
    t, d = h.shape
    tm = min(512, t)
    row = lambda i: (i, 0)
    full = lambda i: (0, 0)
    return pl.pallas_call(
        _outproj_kernel,
        grid=(t // tm,),
        in_specs=[pl.BlockSpec((tm, y_nsa.shape[1]), row), pl.BlockSpec((tm, y_sb.shape[1]), row),
                  pl.BlockSpec((tm, y_fox.shape[1]), row), pl.BlockSpec((tm, d), row), pl.BlockSpec((1, d), full),
                  pl.BlockSpec((d, d), full), pl.BlockSpec((1, d), full), pl.BlockSpec((1, d), full)],
        out_specs=pl.BlockSpec((tm, d), row),
        out_shape=jax.ShapeDtypeStruct((t, d), F32),
        compiler_params=_cparams(("parallel",)),
        name="norm_outproj_ln",
    )(y_nsa, y_sb, y_fox, h, gn, w, g, b)


def _moe_kernel(h_ref, rw_ref, rb_ref, wgu_ref, wd_ref, g_ref, b_ref, o_ref, xb_ref, gate_ref, acc_ref):
    e = pl.program_id(1)

    @pl.when(e == 0)
    def _():
        x = h_ref[...]
        xh, xm, _ = _split3(x)
        xb_ref[...] = xh
        w_hi = rw_ref[0]
        w_lo = rw_ref[1]
        logits = _dot(xh, w_hi) + _dot(xm, w_hi) + _dot(xh, w_lo) + rb_ref[...]
        lane = lax.broadcasted_iota(jnp.int32, logits.shape, 1)
        is_g = lane < N_GROUPS
        gl = jnp.where(is_g, logits, -jnp.inf)
        gmax = jnp.max(gl, axis=-1, keepdims=True)
        g_sel = jnp.min(jnp.where(gl == gmax, lane, LANES), axis=-1, keepdims=True)
        g_w = 1.0 / jnp.sum(jnp.where(is_g, jnp.exp(gl - gmax), 0.0), axis=-1, keepdims=True)
        ex = lane - N_GROUPS
        in_grp = (ex >= 0) & (ex < N_EXPERTS) & (ex // EXPERTS_PER_GROUP == g_sel)
        el = jnp.where(in_grp, logits, -jnp.inf)
        v1 = jnp.max(el, axis=-1, keepdims=True)
        i1 = jnp.min(jnp.where(el == v1, lane, LANES), axis=-1, keepdims=True)
        el2 = jnp.where(lane == i1, -jnp.inf, el)
        v2 = jnp.max(el2, axis=-1, keepdims=True)
        i2 = jnp.min(jnp.where(el2 == v2, lane, LANES), axis=-1, keepdims=True)
        e2 = jnp.exp(v2 - v1)
        w1 = g_w / (1.0 + e2)
        w2 = g_w * e2 / (1.0 + e2)
        gate_ref[...] = jnp.where(lane == i1, w1, 0.0) + jnp.where(lane == i2, w2, 0.0)
        acc_ref[...] = jnp.zeros_like(acc_ref)

    xb = xb_ref[...]
    gu = _dot(xb, wgu_ref[...])
    lane = lax.broadcasted_iota(jnp.int32, gate_ref.shape, 1)
    gcol = jnp.sum(jnp.where(lane == e + N_GROUPS, gate_ref[...], 0.0), axis=-1, keepdims=True)
    hid = jax.nn.silu(gu[:, :D_FF]) * gu[:, D_FF:]
    acc_ref[...] += _dot((hid * gcol).astype(BF16), wd_ref[...])

    @pl.when(e == N_EXPERTS - 1)
    def _():
        o_ref[...] = _layer_norm(ALPHA * h_ref[...] + acc_ref[...], g_ref[...], b_ref[...])


def _moe(h, rw, rb, wgu, wd, g, b):
    t, d = h.shape
    tm = min(512, t)
    return pl.pallas_call(
        _moe_kernel,
        grid=(t // tm, N_EXPERTS),
        in_specs=[pl.BlockSpec((tm, d), lambda i, e: (i, 0)),
                  pl.BlockSpec((2, d, LANES), lambda i, e: (0, 0, 0)),
                  pl.BlockSpec((1, LANES), lambda i, e: (0, 0)),
                  pl.BlockSpec((None, d, 2 * D_FF), lambda i, e: (e, 0, 0)),
                  pl.BlockSpec((None, D_FF, d), lambda i, e: (e, 0, 0)),
                  pl.BlockSpec((1, d), lambda i, e: (0, 0)),
                  pl.BlockSpec((1, d), lambda i, e: (0, 0))],
        out_specs=pl.BlockSpec((tm, d), lambda i, e: (i, 0)),
        out_shape=jax.ShapeDtypeStruct((t, d), F32),
        scratch_shapes=[pltpu.VMEM((tm, d), BF16), pltpu.VMEM((tm, LANES), F32), pltpu.VMEM((tm, d), F32)],
        compiler_params=_cparams(("parallel", "arbitrary")),
        name="moe_ln",
    )(h, rw, rb, wgu, wd, g, b)


def _cmp_to_slc_matrix(n_cmp_pad, n_slc):
    r, c = SLC_LEN // CMP_STRIDE, CMP_LEN // CMP_STRIDE
    m = np.zeros((n_cmp_pad, n_slc), np.float32)
    for j in range(n_slc):
        for a in range(r):
            for bb in range(c):
                i = r * j + a + bb
                if i < n_cmp_pad - 1:
                    m[i, j] += 1.0
    return m


def _compress_weights(pos, w1, w2):
    eye = jnp.eye(N_NSA_KV, dtype=F32)
    half = CMP_LEN // 2
    cols = N_NSA_KV * HEAD_DIM

    def expand(w):
        return jnp.einsum('lde,gh->lgdhe', w, eye).reshape(half * cols, cols).astype(BF16)

    def tile_pos(p):
        return jnp.broadcast_to(p[:, None, :], (half, N_NSA_KV, HEAD_DIM)).reshape(1, half * cols)

    pos2 = jnp.concatenate([tile_pos(pos[:half]), tile_pos(pos[half:])], axis=0)
    w2bd = jnp.einsum('ef,gh->gehf', w2, eye).reshape(cols, cols).astype(BF16)
    return pos2, expand(w1[:half]), expand(w1[half:]), w2bd


def _mixer(h, w_in, cmp_pos_k, cmp_w1_k, cmp_w2_k, cmp_pos_v, cmp_w1_v, cmp_w2_v, fox_forget_bias,
           norm_nsa, norm_sb, norm_fox, w_out, ln_g, ln_b):
    t = h.shape[0]
    nq = t // NSA_QB
    w_main = w_in[:, _PERM_MAIN].astype(BF16)
    w_tail = jnp.pad(w_in[:, _PERM_TAIL], ((0, 0), (0, LANES - _PERM_TAIL.size))).astype(BF16)
    main, tail = _inproj(h, w_main, w_tail)

    n_rows = t // CMP_STRIDE
    kc = _compress(main[:, 1024:1280].reshape(n_rows, -1), *_compress_weights(cmp_pos_k, cmp_w1_k, cmp_w2_k))
    vc = _compress(main[:, 1280:1536].reshape(n_rows, -1), *_compress_weights(cmp_pos_v, cmp_w1_v, cmp_w2_v))
    kvc = jnp.concatenate([kc.reshape(n_rows, N_NSA_KV, HEAD_DIM), vc.reshape(n_rows, N_NSA_KV, HEAD_DIM)],
                          axis=-1).transpose(1, 0, 2)
    slopes = 2.0 **(-8.0 * jnp.arange(1, N_NSA_HEADS + 1, dtype=F32) / N_NSA_HEADS)
    slopes = jnp.repeat(slopes.reshape(N_NSA_KV, NSA_HPG), NSA_QB, axis=1)[..., None]
    imap = jnp.asarray(_cmp_to_slc_matrix(n_rows, t // SLC_LEN), BF16)
    oc, sel, flags = _nsa_cmp(main, kvc, imap, slopes)
    n_chunks = t // SLC_CHUNK
    diag = (jnp.arange(nq) * NSA_QB) // SLC_CHUNK
    active = flags[:, :, 0, :n_chunks] * (jnp.arange(n_chunks)[None, None, :] != diag[None, :, None])
    idx = jnp.argsort(1 - active, axis=-1, stable=True).astype(jnp.int32).reshape(-1)
    cnt = jnp.sum(active, axis=-1).astype(jnp.int32).reshape(-1)
    gates = tail[:, :48].reshape(nq, NSA_QB, N_NSA_KV, NSA_HPG, 3).transpose(2, 0, 3, 1, 4)
    gates = gates.reshape(N_NSA_KV, nq, NSA_HPG * NSA_QB, 3)
    o_nsa = _nsa_main(idx, cnt, main, sel, oc, gates, slopes)

    o_sb = _stick_breaking(main)

    nb = t // LANES
    ff_rows = tail[:, 48:56].T.reshape(N_FOX_HEADS * nb, LANES)
    bias_rows = jnp.repeat(fox_forget_bias, nb)[:, None]
    c = _forget_cumsum(ff_rows, bias_rows, nb).reshape(N_FOX_HEADS, t)
    cend = c[:, FOX_TK - 1::FOX_TK].reshape(-1)
    o_fox = _forgetting(main, c.reshape(N_FOX_HEADS // 2, 2, t), cend)

    gn = jnp.concatenate([norm_nsa, norm_sb, norm_fox])[_PERM_MIX][None, :]
    return _outproj(o_nsa, o_sb, o_fox, h, gn, w_out[_PERM_MIX].astype(BF16), ln_g[None, :], ln_b[None, :])


def _ffn(h, rg_w, rg_b, re_w, re_b, w_gate, w_up, w_down, ln_g, ln_b):
    rw = jnp.pad(jnp.concatenate([rg_w, re_w], axis=1), ((0, 0), (0, LANES - N_GROUPS - N_EXPERTS)))
    rw_hi = rw.astype(BF16)
    rw_lo = (rw - rw_hi.astype(F32)).astype(BF16)
    rb = jnp.pad(jnp.concatenate([rg_b, re_b]), (0, LANES - N_GROUPS - N_EXPERTS))[None, :]
    wgu = jnp.concatenate([w_gate, w_up], axis=-1).astype(BF16)
    return _moe(h, jnp.stack([rw_hi, rw_lo]), rb, wgu, w_down.astype(BF16), ln_g[None, :], ln_b[None, :])


def kernel(x, w_in, cmp_pos_k, cmp_w1_k, cmp_w2_k, cmp_pos_v, cmp_w1_v, cmp_w2_v, fox_forget_bias, norm_nsa, norm_sb, norm_fox, w_out, ln1_g, ln1_b, router_group_w, router_group_b, router_expert_w, router_expert_b, expert_w_gate, expert_w_up, expert_w_down, ln2_g, ln2_b):
    b, t, d = x.shape
    assert b == 1 and d == D_MODEL and t % 1024 == 0
    h = x.reshape(t, d)
    for l in range(DEPTH):
        h = _mixer(h, w_in[l], cmp_pos_k[l], cmp_w1_k[l], cmp_w2_k[l], cmp_pos_v[l], cmp_w1_v[l], cmp_w2_v[l],
                   fox_forget_bias[l], norm_nsa[l], norm_sb[l], norm_fox[l], w_out[l], ln1_g[l], ln1_b[l])
        h = _ffn(h, router_group_w[l], router_group_b[l], router_expert_w[l], router_expert_b[l],
                 expert_w_gate[l], expert_w_up[l], expert_w_down[l], ln2_g[l], ln2_b[l])
    return h.reshape(b, t, d)
```

```python
import functools

import numpy as np
import jax
import jax.numpy as jnp
from jax import lax
from jax.experimental import pallas as pl
from jax.experimental.pallas import tpu as pltpu

F32 = jnp.float32
BF16 = jnp.bfloat16

D_MODEL = 2048
DEPTH = 2
HEAD_DIM = 64
N_NSA_HEADS = 16
N_NSA_KV = 4
NSA_HPG = 4
N_SB_HEADS = 8
N_FOX_HEADS = 8
CMP_LEN = 32
CMP_STRIDE = 16
SLC_LEN = 64
SLC_TOPK = 8
WINDOW = 512
N_GROUPS = 4
EXPERTS_PER_GROUP = 4
N_EXPERTS = 16
D_FF = D_MODEL // 8
ALPHA = (2 * DEPTH) ** 0.25
EPS = 1e-5
NEG_BIG = -1e30
SEL_BONUS = 1e6
QK_SCALE = HEAD_DIM ** -0.5

LANES = 128
NSA_QB = 128
SLC_CHUNK = 128
BLK_PER_CHUNK = SLC_CHUNK // SLC_LEN
SLC_PIECES = 4
MOE_EXPERTS_PER_STEP = 2
SB_BLK = 256
FOX_TQ = 512
FOX_TK = 512
EXP_UNDERFLOW = -104.0
BOUND_SLACK = 6.0
SLC_BLK0, WIN_BLK0 = 1536 // 128, 2048 // 128
SB_Q_BLK0, SB_KV_BLK0 = 2560 // 128, 3072 // 128
FOX_Q_BLK0, FOX_KV_BLK0 = 4096 // 128, 4608 // 128
MAIN_W = 5632
VMEM_LIMIT = 56 * 1024 * 1024

_PROJ_SIZES = (1024, 256, 256, 256, 256, 256, 256, 48, 512, 512, 512, 512, 512, 512, 8)
_OFF = np.concatenate([[0], np.cumsum(_PROJ_SIZES)])


def _perm_columns():
    seg = lambda i: np.arange(_OFF[i], _OFF[i + 1])
    nq, ck, cv, sk, sv, wk, wv, ng, sbq, sbk, sbv, fq, fk, fv, ff = [seg(i) for i in range(15)]

    def interleave(k, v, n, swap_odd=False):
        out = []
        for h in range(n):
            kh, vh = k[h * 64:(h + 1) * 64], v[h * 64:(h + 1) * 64]
            out.append(np.concatenate([vh, kh] if (swap_odd and h % 2) else [kh, vh]))
        return np.concatenate(out)

    main = np.concatenate([nq, ck, cv, interleave(sk, sv, 4), interleave(wk, wv, 4), sbq,
                           interleave(sbk, sbv, 8, True), fq, interleave(fk, fv, 8, True)])
    tail = np.concatenate([ng, ff])
    assert main.size == MAIN_W
    return main, tail


_PERM_MAIN, _PERM_TAIL = _perm_columns()


def _perm_mixer_columns():
    def pairs(base, n_heads):
        order = [h for p in range(n_heads // 2) for h in (2 * p + 1, 2 * p)]
        return np.concatenate([base + h * HEAD_DIM + np.arange(HEAD_DIM) for h in order])

    nsa_w = N_NSA_HEADS * HEAD_DIM
    sb_w = N_SB_HEADS * HEAD_DIM
    return np.concatenate([pairs(0, N_NSA_HEADS), pairs(nsa_w, N_SB_HEADS), pairs(nsa_w + sb_w, N_FOX_HEADS)])


_PERM_MIX = _perm_mixer_columns()


def _cparams(sem):
    return pltpu.CompilerParams(dimension_semantics=sem, vmem_limit_bytes=VMEM_LIMIT)


def _dot(a, b):
    return jnp.dot(a, b, preferred_element_type=F32)


def _dot_nt(a, b):
    return lax.dot_general(a, b, (((1,), (1,)), ((), ())), preferred_element_type=F32)


def _split3(x):
    hi = x.astype(BF16)
    r = x - hi.astype(F32)
    mid = r.astype(BF16)
    lo = (r - mid.astype(F32)).astype(BF16)
    return hi, mid, lo


def _dot_split(x, w, parts):
    pieces = _split3(x)[:parts]
    out = _dot(pieces[0], w)
    for p in pieces[1:]:
        out = out + _dot(p, w)
    return out


def _softplus(z):
    return jnp.maximum(z, 0.0) + jnp.log1p(jnp.exp(-jnp.abs(z)))


def _inproj_kernel(x_ref, w_ref, wt_ref, o_ref, ot_ref, xb_ref):
    @pl.when(pl.program_id(1) == 0)
    def _():
        xb = x_ref[...].astype(BF16)
        xb_ref[...] = xb
        ot_ref[...] = _dot(xb, wt_ref[...])

    o_ref[...] = _dot(xb_ref[...], w_ref[...]).astype(BF16)


def _inproj(h, w_main, w_tail):
    t, d = h.shape
    tm, tn = min(1024, t), 512
    return pl.pallas_call(
        _inproj_kernel,
        grid=(t // tm, MAIN_W // tn),
        in_specs=[pl.BlockSpec((tm, d), lambda i, j: (i, 0)),
                  pl.BlockSpec((d, tn), lambda i, j: (0, j)),
                  pl.BlockSpec((d, LANES), lambda i, j: (0, 0))],
        out_specs=[pl.BlockSpec((tm, tn), lambda i, j: (i, j)),
                   pl.BlockSpec((tm, LANES), lambda i, j: (i, 0))],
        out_shape=[jax.ShapeDtypeStruct((t, MAIN_W), BF16), jax.ShapeDtypeStruct((t, LANES), F32)],
        scratch_shapes=[pltpu.VMEM((tm, d), BF16)],
        compiler_params=_cparams(("parallel", "arbitrary")),
        name="inproj",
    )(h, w_main, w_tail)


def _cumsum_kernel(ff_ref, b_ref, c_ref, *, nb):
    x = ff_ref[...] + b_ref[...]
    lf = -_softplus(-x)
    r = lf.shape[0]
    jj = lax.broadcasted_iota(jnp.int32, (LANES, LANES), 0)
    kk = lax.broadcasted_iota(jnp.int32, (LANES, LANES), 1)
    upper = jnp.where(jj <= kk, 1.0, 0.0).astype(BF16)
    ones = jnp.ones((LANES, LANES), BF16)
    within = _dot_split(lf, upper, 3)
    tot = _dot_split(lf, ones, 3)
    rr = lax.broadcasted_iota(jnp.int32, (r, r), 0)
    cc = lax.broadcasted_iota(jnp.int32, (r, r), 1)
    earlier = jnp.where((cc < rr) & (cc // nb == rr // nb), 1.0, 0.0).astype(BF16)
    hi, mid, lo = _split3(tot)
    off = _dot(earlier, hi) + _dot(earlier, mid) + _dot(earlier, lo)
    c_ref[...] = within + off


def _forget_cumsum(ff_rows, bias_rows, nb):
    r = ff_rows.shape[0]
    return pl.pallas_call(
        functools.partial(_cumsum_kernel, nb=nb),
        out_shape=jax.ShapeDtypeStruct((r, LANES), F32),
        compiler_params=pltpu.CompilerParams(vmem_limit_bytes=VMEM_LIMIT),
        name="forget_cumsum",
    )(ff_rows, bias_rows)


def _gelu_tanh(x):
    return 0.5 * x * (1.0 + jnp.tanh(np.sqrt(2.0 / np.pi).astype(np.float32) * (x + 0.044715 * (x * x * x))))


def _compress_kernel(x_ref, pos_ref, wa_ref, wb_ref, w2_ref, o_ref):
    x = x_ref[...].astype(F32)
    n = x.shape[0]
    first = _dot((x + pos_ref[0:1, :]).astype(BF16), wa_ref[...])
    second = _dot((x + pos_ref[1:2, :]).astype(BF16), wb_ref[...])
    pre = first + pltpu.roll(second, n - 1, 0)
    o_ref[...] = _dot(_gelu_tanh(pre).astype(BF16), w2_ref[...]).astype(BF16)


def _compress(x2, pos2, wa, wb, w2):
    n = x2.shape[0]
    return pl.pallas_call(
        _compress_kernel,
        out_shape=jax.ShapeDtypeStruct((n, N_NSA_KV * HEAD_DIM), BF16),
        compiler_params=pltpu.CompilerParams(vmem_limit_bytes=VMEM_LIMIT),
        name="nsa_compress",
    )(x2, pos2, wa, wb, w2)


def _group_queries(q_ref):
    q = q_ref[...]
    low = lax.broadcasted_iota(jnp.int32, (NSA_QB, LANES), 1) < HEAD_DIM
    zero = jnp.zeros((NSA_QB, LANES), BF16)
    out = []
    for j in range(NSA_HPG // 2):
        pair = q[:, j * LANES:(j + 1) * LANES]
        swapped = pltpu.roll(pair.astype(F32), HEAD_DIM, 1).astype(BF16)
        out += [jnp.where(low, pair, zero), jnp.where(low, swapped, zero)]
    return jnp.concatenate(out, axis=0) * QK_SCALE


def _head_sum(p):
    return p[0:NSA_QB] + p[NSA_QB:2 * NSA_QB] + p[2 * NSA_QB:3 * NSA_QB] + p[3 * NSA_QB:4 * NSA_QB]


def _nsa_cmp_kernel(q_ref, kvc_ref, imap_ref, slope_ref, oc_ref, sel_ref, flag_ref, s_ref, *, chunk):
    t0 = pl.program_id(1) * NSA_QB
    rows = NSA_HPG * NSA_QB
    q = _group_queries(q_ref)
    slope = slope_ref[...]
    n_slc = sel_ref.shape[-1]
    trow = lax.broadcasted_iota(jnp.int32, (rows, 1), 0) % NSA_QB
    n_valid = t0 // CMP_STRIDE + (NSA_QB - CMP_LEN) // CMP_STRIDE + 1
    n_steps = (n_valid + chunk - 1) // chunk
    col = lax.broadcasted_iota(jnp.int32, (1, chunk), 1)

    def logits(c, m):
        n0 = pl.multiple_of(c * chunk, chunk)
        end_rel = (n0 + col) * CMP_STRIDE + (CMP_LEN - 1) - t0
        s = _dot_nt(q, kvc_ref[pl.ds(n0, chunk), :]) + slope * end_rel.astype(F32)
        s = jnp.where(end_rel <= trow, s, NEG_BIG)
        s_ref[:, pl.ds(n0, chunk)] = s
        return jnp.maximum(m, jnp.max(s, axis=-1, keepdims=True))

    m = lax.fori_loop(0, n_steps, logits, jnp.full((rows, 1), NEG_BIG, F32))
    m = jnp.where(m > 0.5 * NEG_BIG, m, 0.0)
    lane = lax.broadcasted_iota(jnp.int32, (chunk, LANES), 1)

    def weights(c, carry):
        acc, imp = carry
        n0 = pl.multiple_of(c * chunk, chunk)
        e = jnp.exp(s_ref[:, pl.ds(n0, chunk)] - m).astype(BF16)
        kvc = kvc_ref[pl.ds(n0, chunk), :]
        acc = acc + _dot(e, jnp.where(lane == 0, jnp.ones_like(kvc), kvc))
        return acc, imp + _dot(e, imap_ref[pl.ds(n0, chunk), :])

    acc, imp = lax.fori_loop(0, n_steps, weights,
                             (jnp.zeros((rows, LANES), F32), jnp.zeros((rows, n_slc), F32)))
    l = acc[:, 0:1]
    inv = 1.0 / jnp.where(l > 0.0, l, 1.0)
    oc_ref[...] = (acc * inv).reshape(NSA_HPG, NSA_QB, LANES)
    imp = _head_sum(imp * inv)

    tq = t0 + lax.broadcasted_iota(jnp.int32, (1, NSA_QB), 1)
    jb = lax.broadcasted_iota(jnp.int32, (n_slc, 1), 0)
    jbf = jb.astype(F32)
    cur = tq // SLC_LEN
    forced = (jb == 0) | (jb == cur) | (jb == cur - 1)
    valid = jb * SLC_LEN <= tq
    score = jnp.where(valid, imp.T + jnp.where(forced, SEL_BONUS, 0.0), -jnp.inf)
    sel = jnp.zeros((n_slc, NSA_QB), F32)
    for _ in range(SLC_TOPK):
        best = jnp.max(score, axis=0, keepdims=True)
        first = jnp.min(jnp.where(score == best, jbf, float(n_slc)), axis=0, keepdims=True)
        hit = jbf == first
        sel = jnp.where(hit, 1.0, sel)
        score = jnp.where(hit, -jnp.inf, score)
    sel = jnp.where(valid, sel, 0.0).T
    sel_ref[...] = sel.astype(BF16)
    any_q = jnp.max(sel, axis=0, keepdims=True)
    jr = lax.broadcasted_iota(jnp.int32, (n_slc, LANES), 0)
    cr = lax.broadcasted_iota(jnp.int32, (n_slc, LANES), 1)
    group = jnp.where(jr // BLK_PER_CHUNK == cr, 1.0, 0.0).astype(BF16)
    per_chunk = _dot(jnp.broadcast_to(any_q, (8, n_slc)).astype(BF16), group)
    flag_ref[...] = jnp.where(per_chunk > 0.5, 1, 0).astype(jnp.int32)


def _nsa_cmp(main, kvc, imap, slopes):
    t = main.shape[0]
    nq = t // NSA_QB
    n = kvc.shape[1]
    n_slc = t // SLC_LEN
    rows = NSA_HPG * NSA_QB
    chunk = min(256, n)
    return pl.pallas_call(
        functools.partial(_nsa_cmp_kernel, chunk=chunk),
        grid=(N_NSA_KV, nq),
        in_specs=[pl.BlockSpec((NSA_QB, NSA_HPG * HEAD_DIM), lambda g, i: (i, g)),
                  pl.BlockSpec((None, n, LANES), lambda g, i: (g, 0, 0)),
                  pl.BlockSpec((n, n_slc), lambda g, i: (0, 0)),
                  pl.BlockSpec((None, rows, 1), lambda g, i: (g, 0, 0))],
        out_specs=[pl.BlockSpec((NSA_HPG, NSA_QB, LANES), lambda g, i: (g, i, 0)),
                   pl.BlockSpec((None, NSA_QB, n_slc), lambda g, i: (g, i, 0)),
                   pl.BlockSpec((None, None, 8, LANES), lambda g, i: (g, i, 0, 0))],
        out_shape=[jax.ShapeDtypeStruct((N_NSA_HEADS, t, LANES), F32),
                   jax.ShapeDtypeStruct((N_NSA_KV, t, n_slc), BF16),
                   jax.ShapeDtypeStruct((N_NSA_KV, nq, 8, LANES), jnp.int32)],
        scratch_shapes=[pltpu.VMEM((rows, n), F32)],
        compiler_params=_cparams(("parallel", "arbitrary")),
        name="nsa_cmp_select",
    )(main, kvc, imap, slopes)


def _nsa_main_kernel(idx_ref, cnt_ref, q_ref, slc_ref, win_ref, sel_ref, oc_ref, gate_ref, slope_ref, o_ref,
                     *, n_chunks):
    g = pl.program_id(0)
    qb = pl.program_id(1)
    nq = pl.num_programs(1)
    t0 = qb * NSA_QB
    rows = NSA_HPG * NSA_QB
    q = _group_queries(q_ref)
    slope = slope_ref[...]
    trow = lax.broadcasted_iota(jnp.int32, (rows, 1), 0) % NSA_QB
    n_slc = sel_ref.shape[-1]
    sel = sel_ref[...]

    def with_ones(kv):
        lane = lax.broadcasted_iota(jnp.int32, kv.shape, 1)
        return jnp.where(lane == 0, jnp.ones_like(kv), kv)

    band = WINDOW + NSA_QB
    start = pl.multiple_of(jnp.maximum(t0 - WINDOW, 0), NSA_QB)
    kw = win_ref[pl.ds(start, band), :]
    krel = (start - t0) + lax.broadcasted_iota(jnp.int32, (1, band), 1)
    dist = trow - krel
    ok = (dist >= 0) & (dist < WINDOW)
    s = jnp.where(ok, _dot_nt(q, kw) + slope * krel.astype(F32), NEG_BIG)
    m = jnp.max(s, axis=-1, keepdims=True)
    e = jnp.where(ok, jnp.exp(s - m), 0.0)
    o_w = _dot(e.astype(BF16), with_ones(kw))
    o_w = o_w / o_w[:, 0:1]

    def picked_keys(block_of_key):
        jr = lax.broadcasted_iota(jnp.int32, (n_slc, block_of_key.shape[1]), 0)
        expand = jnp.where(jr == block_of_key, 1.0, 0.0).astype(BF16)
        return jnp.concatenate([_dot(sel, expand)] * NSA_HPG, axis=0)

    cd = t0 // SLC_CHUNK
    k0 = pl.multiple_of(cd * SLC_CHUNK, SLC_CHUNK)
    kv = slc_ref[pl.ds(k0, SLC_CHUNK), :]
    cr = lax.broadcasted_iota(jnp.int32, (1, SLC_CHUNK), 1)
    krel = (k0 - t0) + cr
    okc = (picked_keys(cd * BLK_PER_CHUNK + cr // SLC_LEN) > 0.5) & (krel <= trow)
    s = jnp.where(okc, _dot_nt(q, kv) + slope * krel.astype(F32), NEG_BIG)
    m = jnp.max(s, axis=-1, keepdims=True)
    p = jnp.where(okc, jnp.exp(s - m), 0.0)
    acc = _dot(p.astype(BF16), with_ones(kv))

    base = (g * nq + qb) * n_chunks
    cnt = cnt_ref[g * nq + qb]
    crp = lax.broadcasted_iota(jnp.int32, (1, SLC_PIECES * SLC_CHUNK), 1)
    piece = crp // SLC_CHUNK
    in_piece = crp - piece * SLC_CHUNK

    def gathered_chunks(i, carry):
        m, acc = carry
        first = SLC_PIECES * i
        chunk_of_key = jnp.zeros_like(crp)
        live = jnp.zeros_like(crp)
        parts = []
        for j in range(SLC_PIECES):
            has = first + j < cnt
            cj = idx_ref[base + jnp.where(has, first + j, first)]
            parts.append(slc_ref[pl.ds(pl.multiple_of(cj * SLC_CHUNK, SLC_CHUNK), SLC_CHUNK), :])
            chunk_of_key = jnp.where(piece == j, cj, chunk_of_key)
            live = jnp.where(piece == j, has.astype(jnp.int32), live)
        kv = jnp.concatenate(parts, axis=0)
        block_of_key = jnp.where(live > 0, chunk_of_key * BLK_PER_CHUNK + in_piece // SLC_LEN, -1)
        krel = (chunk_of_key * SLC_CHUNK + in_piece - t0).astype(F32)
        s = _dot_nt(q, kv) + slope * krel + (picked_keys(block_of_key) - 1.0) * (-NEG_BIG)
        m_new = jnp.maximum(m, jnp.max(s, axis=-1, keepdims=True))
        p = jnp.exp(s - m_new)
        return m_new, jnp.exp(m - m_new) * acc + _dot(p.astype(BF16), with_ones(kv))

    m, acc = lax.fori_loop(0, (cnt + SLC_PIECES - 1) // SLC_PIECES, gathered_chunks, (m, acc))
    o_s = acc / acc[:, 0:1]

    gt = jax.nn.sigmoid(gate_ref[...])
    o_c = oc_ref[...].reshape(rows, LANES)
    o = gt[:, 0:1] * o_c + gt[:, 1:2] * o_s + gt[:, 2:3] * o_w
    low = lax.broadcasted_iota(jnp.int32, (NSA_QB, LANES), 1) < HEAD_DIM
    pairs = []
    for j in range(NSA_HPG // 2):
        even = o[2 * j * NSA_QB:(2 * j + 1) * NSA_QB]
        odd = o[(2 * j + 1) * NSA_QB:(2 * j + 2) * NSA_QB]
        pairs.append(jnp.where(low, pltpu.roll(odd, HEAD_DIM, 1), even))
    o_ref[...] = jnp.concatenate(pairs, axis=1).astype(BF16)


def _nsa_main(idx, cnt, main, sel, oc, gates, slopes):
    t = main.shape[0]
    nq = t // NSA_QB
    n_slc = t // SLC_LEN
    n_chunks = t // SLC_CHUNK
    rows = NSA_HPG * NSA_QB
    width = NSA_HPG * HEAD_DIM
    grid_spec = pltpu.PrefetchScalarGridSpec(
        num_scalar_prefetch=2,
        grid=(N_NSA_KV, nq),
        in_specs=[pl.BlockSpec((NSA_QB, width), lambda g, i, ix, ct: (i, g)),
                  pl.BlockSpec((t, LANES), lambda g, i, ix, ct: (0, SLC_BLK0 + g)),
                  pl.BlockSpec((t, LANES), lambda g, i, ix, ct: (0, WIN_BLK0 + g)),
                  pl.BlockSpec((None, NSA_QB, n_slc), lambda g, i, ix, ct: (g, i, 0)),
                  pl.BlockSpec((NSA_HPG, NSA_QB, LANES), lambda g, i, ix, ct: (g, i, 0)),
                  pl.BlockSpec((None, None, rows, 3), lambda g, i, ix, ct: (g, i, 0, 0)),
                  pl.BlockSpec((None, rows, 1), lambda g, i, ix, ct: (g, 0, 0))],
        out_specs=pl.BlockSpec((NSA_QB, width), lambda g, i, ix, ct: (i, g)),
    )
    return pl.pallas_call(
        functools.partial(_nsa_main_kernel, n_chunks=n_chunks),
        grid_spec=grid_spec,
        out_shape=jax.ShapeDtypeStruct((t, N_NSA_HEADS * HEAD_DIM), BF16),
        compiler_params=_cparams(("parallel", "arbitrary")),
        name="nsa_select_window",
    )(idx, cnt, main, main, main, sel, oc, gates, slopes)


def _pair_queries(q_ref):
    q = q_ref[...]
    low = lax.broadcasted_iota(jnp.int32, q.shape, 1) < HEAD_DIM
    zero = jnp.zeros_like(q)
    return jnp.where(low, q, zero) * QK_SCALE, jnp.where(low, zero, q) * QK_SCALE, low


def _sb_kernel(q_ref, kve_ref, kvo_ref, o_ref):
    qi = pl.program_id(1)
    q_e, q_o, low = _pair_queries(q_ref)
    rr = lax.broadcasted_iota(jnp.int32, (SB_BLK, SB_BLK), 0)
    cc = lax.broadcasted_iota(jnp.int32, (SB_BLK, SB_BLK), 1)
    later = jnp.where(rr > cc, 1.0, 0.0).astype(BF16)
    ones = jnp.ones((SB_BLK, LANES), BF16)

    def head_block(q, kv_ref, kb, mask, run, acc):
        kv = kv_ref[pl.ds(pl.multiple_of(kb * SB_BLK, SB_BLK), SB_BLK), :]
        z = _dot_nt(q, kv)
        sp = _softplus(z)
        log_1mb = -sp if mask is None else jnp.where(mask, -sp, 0.0)
        hi, lo = _split3(log_1mb)[:2]
        within = _dot(hi, later) + _dot(lo, later)
        total = _dot(hi, ones) + _dot(lo, ones)
        a = jnp.exp(z - sp + within + run[:, 0:1])
        if mask is not None:
            a = jnp.where(mask, a, 0.0)
        return run + total, acc + _dot(a.astype(BF16), kv)

    def both(kb, mask, st):
        run_e, acc_e, run_o, acc_o = st
        run_e, acc_e = head_block(q_e, kve_ref, kb, mask, run_e, acc_e)
        run_o, acc_o = head_block(q_o, kvo_ref, kb, mask, run_o, acc_o)
        return run_e, acc_e, run_o, acc_o

    def alive(st):
        return (jnp.max(jnp.maximum(st[0], st[2])) > EXP_UNDERFLOW).astype(jnp.int32)

    zero = jnp.zeros((SB_BLK, LANES), F32)
    st = both(qi, cc < rr, (zero, zero, zero, zero))

    def cond(c):
        return (c[0] >= 0) & (c[1] > 0)

    def body(c):
        st = both(c[0], None, c[2])
        return c[0] - 1, alive(st), st

    _, _, st = lax.while_loop(cond, body, (qi - 1, alive(st), st))
    o_ref[...] = jnp.where(low, st[3], st[1]).astype(BF16)


def _stick_breaking(main):
    t = main.shape[0]
    return pl.pallas_call(
        _sb_kernel,
        grid=(N_SB_HEADS // 2, t // SB_BLK),
        in_specs=[pl.BlockSpec((SB_BLK, LANES), lambda p, i: (i, SB_Q_BLK0 + p)),
                  pl.BlockSpec((t, LANES), lambda p, i: (0, SB_KV_BLK0 + 2 * p)),
                  pl.BlockSpec((t, LANES), lambda p, i: (0, SB_KV_BLK0 + 2 * p + 1))],
        out_specs=pl.BlockSpec((SB_BLK, LANES), lambda p, i: (i, p)),
        out_shape=jax.ShapeDtypeStruct((t, N_SB_HEADS * HEAD_DIM), BF16),
        compiler_params=_cparams(("parallel", "arbitrary")),
        name="stick_breaking",
    )(main, main, main)


def _fox_kernel(cend_ref, q_ref, kve_ref, kvo_ref, c_ref, o_ref, kmax_ref):
    pair = pl.program_id(0)
    qi = pl.program_id(1)
    ratio = FOX_TQ // FOX_TK
    nkb = pl.num_programs(1) * ratio
    t = kve_ref.shape[0]
    scan = 1024

    @pl.when(qi == 0)
    def _():
        def max_norm(ref, keys_low):
            keep = (lax.broadcasted_iota(jnp.int32, (scan, LANES), 1) < HEAD_DIM) == keys_low

            def step(r, best):
                k = ref[pl.ds(pl.multiple_of(r * scan, scan), scan), :].astype(F32)
                k = jnp.where(keep, k, 0.0)
                return jnp.maximum(best, jnp.sum(k * k, axis=-1, keepdims=True))

            best = lax.fori_loop(0, t // scan, step, jnp.zeros((scan, 1), F32))
            return jnp.sqrt(jnp.max(best, axis=0, keepdims=True))

        kmax_ref[0:1, :] = jnp.broadcast_to(max_norm(kve_ref, True), (1, LANES))
        kmax_ref[1:2, :] = jnp.broadcast_to(max_norm(kvo_ref, False), (1, LANES))

    q_e, q_o, low = _pair_queries(q_ref)

    def logit_bound(q, row):
        qf = q.astype(F32)
        return jnp.sqrt(jnp.sum(qf * qf, axis=-1, keepdims=True)) * kmax_ref[row:row + 1, 0:1]

    bound = (logit_bound(q_e, 0), logit_bound(q_o, 1))
    rr = lax.broadcasted_iota(jnp.int32, (FOX_TQ, FOX_TK), 0)
    cc = lax.broadcasted_iota(jnp.int32, (FOX_TQ, FOX_TK), 1)

    key_lane = lax.broadcasted_iota(jnp.int32, (FOX_TK, LANES), 1)
    one_bf = jnp.ones((FOX_TK, LANES), BF16)

    def block_keys(kv_ref, kb):
        return kv_ref[pl.ds(pl.multiple_of(kb * FOX_TK, FOX_TK), FOX_TK), :]

    def scores(q, kv_ref, row, kb):
        k0 = pl.multiple_of(kb * FOX_TK, FOX_TK)
        return _dot_nt(q, block_keys(kv_ref, kb)) - c_ref[row:row + 1, pl.ds(k0, FOX_TK)]

    def consume(s, kv_ref, row, kb, mask, st):
        m, acc = st
        if mask is not None:
            s = jnp.where(mask, s, NEG_BIG)
        m_new = jnp.maximum(m, jnp.max(s, axis=-1, keepdims=True))
        p = jnp.exp(s - m_new)
        if mask is not None:
            p = jnp.where(mask, p, 0.0)
        vals = jnp.where(key_lane == row * HEAD_DIM, one_bf, block_keys(kv_ref, kb))
        return m_new, jnp.exp(m - m_new) * acc + _dot(p.astype(BF16), vals)

    def both_scores(kb):
        return scores(q_e, kve_ref, 0, kb), scores(q_o, kvo_ref, 1, kb)

    def both(s, kb, mask, st):
        return (consume(s[0], kve_ref, 0, kb, mask, st[0]), consume(s[1], kvo_ref, 1, kb, mask, st[1]))

    def needed(kb, st):
        kb = jnp.maximum(kb, 0)
        go = jnp.zeros((), jnp.bool_)
        for row in (0, 1):
            top = jnp.max(bound[row] - st[row][0])
            go = go | (top - cend_ref[(2 * pair + row) * nkb + kb] >= EXP_UNDERFLOW - BOUND_SLACK)
        return go.astype(jnp.int32)

    one = (jnp.full((FOX_TQ, 1), NEG_BIG, F32), jnp.zeros((FOX_TQ, LANES), F32))
    st = (one, one)
    for j in reversed(range(ratio)):
        kb = qi * ratio + j
        st = both(both_scores(kb), kb, rr - cc >= j * FOX_TK, st)

    def cond(c):
        return (c[0] >= 0) & (c[1] > 0)

    def body(c):
        st = both(both_scores(c[0]), c[0], None, c[2])
        return c[0] - 1, needed(c[0] - 1, st), st

    kb0 = qi * ratio - 1
    _, _, st = lax.while_loop(cond, body, (kb0, needed(kb0, st), st))
    (_, acc_e), (_, acc_o) = st
    o_e = acc_e / acc_e[:, 0:1]
    o_o = acc_o / acc_o[:, HEAD_DIM:HEAD_DIM + 1]
    o_ref[...] = jnp.where(low, o_o, o_e).astype(BF16)


def _forgetting(main, c, cend):
    t = main.shape[0]
    grid_spec = pltpu.PrefetchScalarGridSpec(
        num_scalar_prefetch=1,
        grid=(N_FOX_HEADS // 2, t // FOX_TQ),
        in_specs=[pl.BlockSpec((FOX_TQ, LANES), lambda p, i, ce: (i, FOX_Q_BLK0 + p)),
                  pl.BlockSpec((t, LANES), lambda p, i, ce: (0, FOX_KV_BLK0 + 2 * p)),
                  pl.BlockSpec((t, LANES), lambda p, i, ce: (0, FOX_KV_BLK0 + 2 * p + 1)),
                  pl.BlockSpec((None, 2, t), lambda p, i, ce: (p, 0, 0))],
        out_specs=pl.BlockSpec((FOX_TQ, LANES), lambda p, i, ce: (i, p)),
        scratch_shapes=[pltpu.VMEM((8, LANES), F32)],
    )
    return pl.pallas_call(
        _fox_kernel,
        grid_spec=grid_spec,
        out_shape=jax.ShapeDtypeStruct((t, N_FOX_HEADS * HEAD_DIM), BF16),
        compiler_params=_cparams(("parallel", "arbitrary")),
        name="forgetting_attention",
    )(cend, main, main, main, c)


def _layer_norm(x, g, b):
    mu = jnp.mean(x, axis=-1, keepdims=True)
    xc = x - mu
    var = jnp.mean(xc * xc, axis=-1, keepdims=True)
    return xc * lax.rsqrt(var + EPS) * g + b


def _outproj_kernel(yn_ref, ys_ref, yf_ref, h_ref, gn_ref, w_ref, g_ref, b_ref, o_ref):
    gn = gn_ref[...]
    parts = []
    lo = 0
    for y_ref in (yn_ref, ys_ref, yf_ref):
        yp = y_ref[...].astype(F32)
        hi = lo + yp.shape[1]
        ms = jnp.mean(yp * yp, axis=-1, keepdims=True)
        parts.append((yp * lax.rsqrt(ms + EPS) * gn[:, lo:hi]).astype(BF16))
        lo = hi
    mix = _dot(jnp.concatenate(parts, axis=-1), w_ref[...])
    o_ref[...] = _layer_norm(ALPHA * h_ref[...] + mix, g_ref[...], b_ref[...])


def _outproj(y_nsa, y_sb, y_fox, h, gn, w, g, b):
    t, d = h.shape
    tm = min(512, t)
    row = lambda i: (i, 0)
    full = lambda i: (0, 0)
    return pl.pallas_call(
        _outproj_kernel,
        grid=(t // tm,),
        in_specs=[pl.BlockSpec((tm, y_nsa.shape[1]), row), pl.BlockSpec((tm, y_sb.shape[1]), row),
                  pl.BlockSpec((tm, y_fox.shape[1]), row), pl.BlockSpec((tm, d), row), pl.BlockSpec((1, d), full),
                  pl.BlockSpec((d, d), full), pl.BlockSpec((1, d), full), pl.BlockSpec((1, d), full)],
        out_specs=pl.BlockSpec((tm, d), row),
        out_shape=jax.ShapeDtypeStruct((t, d), F32),
        compiler_params=_cparams(("parallel",)),
        name="norm_outproj_ln",
    )(y_nsa, y_sb, y_fox, h, gn, w, g, b)


def _moe_kernel(h_ref, rw_ref, rb_ref, wgu_ref, wd_ref, g_ref, b_ref, o_ref, xb_ref, gate_ref, acc_ref):
    e = pl.program_id(1)

    @pl.when(e == 0)
    def _():
        x = h_ref[...]
        xh, xm, _ = _split3(x)
        xb_ref[...] = xh
        w_hi = rw_ref[0]
        w_lo = rw_ref[1]
        logits = _dot(xh, w_hi) + _dot(xm, w_hi) + _dot(xh, w_lo) + rb_ref[...]
        lane = lax.broadcasted_iota(jnp.int32, logits.shape, 1)
        is_g = lane < N_GROUPS
        gl = jnp.where(is_g, logits, -jnp.inf)
        gmax = jnp.max(gl, axis=-1, keepdims=True)
        g_sel = jnp.min(jnp.where(gl == gmax, lane, LANES), axis=-1, keepdims=True)
        g_w = 1.0 / jnp.sum(jnp.where(is_g, jnp.exp(gl - gmax), 0.0), axis=-1, keepdims=True)
        ex = lane - N_GROUPS
        in_grp = (ex >= 0) & (ex < N_EXPERTS) & (ex // EXPERTS_PER_GROUP == g_sel)
        el = jnp.where(in_grp, logits, -jnp.inf)
        v1 = jnp.max(el, axis=-1, keepdims=True)
        i1 = jnp.min(jnp.where(el == v1, lane, LANES), axis=-1, keepdims=True)
        el2 = jnp.where(lane == i1, -jnp.inf, el)
        v2 = jnp.max(el2, axis=-1, keepdims=True)
        i2 = jnp.min(jnp.where(el2 == v2, lane, LANES), axis=-1, keepdims=True)
        e2 = jnp.exp(v2 - v1)
        w1 = g_w / (1.0 + e2)
        w2 = g_w * e2 / (1.0 + e2)
        gate_ref[...] = jnp.where(lane == i1, w1, 0.0) + jnp.where(lane == i2, w2, 0.0)
        acc_ref[...] = jnp.zeros_like(acc_ref)

    xb = xb_ref[...]
    lane = lax.broadcasted_iota(jnp.int32, gate_ref.shape, 1)
    update = jnp.zeros(acc_ref.shape, F32)
    for j in range(MOE_EXPERTS_PER_STEP):
        gu = _dot(xb, wgu_ref[j])
        expert_lane = e * MOE_EXPERTS_PER_STEP + j + N_GROUPS
        gcol = jnp.sum(jnp.where(lane == expert_lane, gate_ref[...], 0.0), axis=-1, keepdims=True)
        hid = jax.nn.silu(gu[:, :D_FF]) * gu[:, D_FF:]
        update = update + _dot((hid * gcol).astype(BF16), wd_ref[j])
    acc_ref[...] += update

    @pl.when(e == N_EXPERTS // MOE_EXPERTS_PER_STEP - 1)
    def _():
        o_ref[...] = _layer_norm(ALPHA * h_ref[...] + acc_ref[...], g_ref[...], b_ref[...])


def _moe(h, rw, rb, wgu, wd, g, b):
    t, d = h.shape
    tm = min(512, t)
    return pl.pallas_call(
        _moe_kernel,
        grid=(t // tm, N_EXPERTS // MOE_EXPERTS_PER_STEP),
        in_specs=[pl.BlockSpec((tm, d), lambda i, e: (i, 0)),
                  pl.BlockSpec((2, d, LANES), lambda i, e: (0, 0, 0)),
                  pl.BlockSpec((1, LANES), lambda i, e: (0, 0)),
                  pl.BlockSpec((MOE_EXPERTS_PER_STEP, d, 2 * D_FF), lambda i, e: (e, 0, 0)),
                  pl.BlockSpec((MOE_EXPERTS_PER_STEP, D_FF, d), lambda i, e: (e, 0, 0)),
                  pl.BlockSpec((1, d), lambda i, e: (0, 0)),
                  pl.BlockSpec((1, d), lambda i, e: (0, 0))],
        out_specs=pl.BlockSpec((tm, d), lambda i, e: (i, 0)),
        out_shape=jax.ShapeDtypeStruct((t, d), F32),
        scratch_shapes=[pltpu.VMEM((tm, d), BF16), pltpu.VMEM((tm, LANES), F32), pltpu.VMEM((tm, d), F32)],
        compiler_params=_cparams(("parallel", "arbitrary")),
        name="moe_ln",
    )(h, rw, rb, wgu, wd, g, b)


def _cmp_to_slc_matrix(n_cmp_pad, n_slc):
    r, c = SLC_LEN // CMP_STRIDE, CMP_LEN // CMP_STRIDE
    m = np.zeros((n_cmp_pad, n_slc), np.float32)
    for j in range(n_slc):
        for a in range(r):
            for bb in range(c):
                i = r * j + a + bb
                if i < n_cmp_pad - 1:
                    m[i, j] += 1.0
    return m


def _compress_weights(pos, w1, w2):
    eye = jnp.eye(N_NSA_KV, dtype=F32)
    half = CMP_LEN // 2
    cols = N_NSA_KV * HEAD_DIM

    def expand(w):
        return jnp.einsum('lde,gh->lgdhe', w, eye).reshape(half * cols, cols).astype(BF16)

    def tile_pos(p):
        return jnp.broadcast_to(p[:, None, :], (half, N_NSA_KV, HEAD_DIM)).reshape(1, half * cols)

    pos2 = jnp.concatenate([tile_pos(pos[:half]), tile_pos(pos[half:])], axis=0)
    w2bd = jnp.einsum('ef,gh->gehf', w2, eye).reshape(cols, cols).astype(BF16)
    return pos2, expand(w1[:half]), expand(w1[half:]), w2bd


def _mixer(h, w_in, cmp_pos_k, cmp_w1_k, cmp_w2_k, cmp_pos_v, cmp_w1_v, cmp_w2_v, fox_forget_bias,
           norm_nsa, norm_sb, norm_fox, w_out, ln_g, ln_b):
    t = h.shape[0]
    nq = t // NSA_QB
    w_main = w_in[:, _PERM_MAIN].astype(BF16)
    w_tail = jnp.pad(w_in[:, _PERM_TAIL], ((0, 0), (0, LANES - _PERM_TAIL.size))).astype(BF16)
    main, tail = _inproj(h, w_main, w_tail)

    n_rows = t // CMP_STRIDE
    kc = _compress(main[:, 1024:1280].reshape(n_rows, -1), *_compress_weights(cmp_pos_k, cmp_w1_k, cmp_w2_k))
    vc = _compress(main[:, 1280:1536].reshape(n_rows, -1), *_compress_weights(cmp_pos_v, cmp_w1_v, cmp_w2_v))
    kvc = jnp.concatenate([kc.reshape(n_rows, N_NSA_KV, HEAD_DIM), vc.reshape(n_rows, N_NSA_KV, HEAD_DIM)],
                          axis=-1).transpose(1, 0, 2)
    slopes = 2.0 **(-8.0 * jnp.arange(1, N_NSA_HEADS + 1, dtype=F32) / N_NSA_HEADS)
    slopes = jnp.repeat(slopes.reshape(N_NSA_KV, NSA_HPG), NSA_QB, axis=1)[..., None]
    imap = jnp.asarray(_cmp_to_slc_matrix(n_rows, t // SLC_LEN), BF16)
    oc, sel, flags = _nsa_cmp(main, kvc, imap, slopes)
    n_chunks = t // SLC_CHUNK
    diag = (jnp.arange(nq) * NSA_QB) // SLC_CHUNK
    active = flags[:, :, 0, :n_chunks] * (jnp.arange(n_chunks)[None, None, :] != diag[None, :, None])
    idx = jnp.argsort(1 - active, axis=-1, stable=True).astype(jnp.int32).reshape(-1)
    cnt = jnp.sum(active, axis=-1).astype(jnp.int32).reshape(-1)
    gates = tail[:, :48].reshape(nq, NSA_QB, N_NSA_KV, NSA_HPG, 3).transpose(2, 0, 3, 1, 4)
    gates = gates.reshape(N_NSA_KV, nq, NSA_HPG * NSA_QB, 3)
    o_nsa = _nsa_main(idx, cnt, main, sel, oc, gates, slopes)

    o_sb = _stick_breaking(main)

    nb = t // LANES
    ff_rows = tail[:, 48:56].T.reshape(N_FOX_HEADS * nb, LANES)
    bias_rows = jnp.repeat(fox_forget_bias, nb)[:, None]
    c = _forget_cumsum(ff_rows, bias_rows, nb).reshape(N_FOX_HEADS, t)
    cend = c[:, FOX_TK - 1::FOX_TK].reshape(-1)
    o_fox = _forgetting(main, c.reshape(N_FOX_HEADS // 2, 2, t), cend)

    gn = jnp.concatenate([norm_nsa, norm_sb, norm_fox])[_PERM_MIX][None, :]
    return _outproj(o_nsa, o_sb, o_fox, h, gn, w_out[_PERM_MIX].astype(BF16), ln_g[None, :], ln_b[None, :])


def _ffn(h, rg_w, rg_b, re_w, re_b, w_gate, w_up, w_down, ln_g, ln_b):
    rw = jnp.pad(jnp.concatenate([rg_w, re_w], axis=1), ((0, 0), (0, LANES - N_GROUPS - N_EXPERTS)))
    rw_hi = rw.astype(BF16)
    rw_lo = (rw - rw_hi.astype(F32)).astype(BF16)
    rb = jnp.pad(jnp.concatenate([rg_b, re_b]), (0, LANES - N_GROUPS - N_EXPERTS))[None, :]
    wgu = jnp.concatenate([w_gate, w_up], axis=-1).astype(BF16)
    return _moe(h, jnp.stack([rw_hi, rw_lo]), rb, wgu, w_down.astype(BF16), ln_g[None, :], ln_b[None, :])


def kernel(x, w_in, cmp_pos_k, cmp_w1_k, cmp_w2_k, cmp_pos_v, cmp_w1_v, cmp_w2_v, fox_forget_bias, norm_nsa, norm_sb, norm_fox, w_out, ln1_g, ln1_b, router_group_w, router_group_b, router_expert_w, router_expert_b, expert_w_gate, expert_w_up, expert_w_down, ln2_g, ln2_b):
    b, t, d = x.shape
    assert b == 1 and d == D_MODEL and t % 1024 == 0
    h = x.reshape(t, d)
    for l in range(DEPTH):
        h = _mixer(h, w_in[l], cmp_pos_k[l], cmp_w1_k[l], cmp_w2_k[l], cmp_pos_v[l], cmp_w1_v[l], cmp_w2_v[l],
                   fox_forget_bias[l], norm_nsa[l], norm_sb[l], norm_fox[l], w_out[l], ln1_g[l], ln1_b[l])
        h = _ffn(h, router_group_w[l], router_group_b[l], router_expert_w[l], router_expert_b[l],
                 expert_w_gate[l], expert_w_up[l], expert_w_down[l], ln2_g[l], ln2_b[l])
    return h.reshape(b, t, d)
```

```python
import functools

import numpy as np
import jax
import jax.numpy as jnp
from jax import lax
from jax.experimental import pallas as pl
from jax.experimental.pallas import tpu as pltpu

F32 = jnp.float32
BF16 = jnp.bfloat16

D_MODEL = 2048
DEPTH = 2
HEAD_DIM = 64
N_NSA_HEADS = 16
N_NSA_KV = 4
NSA_HPG = 4
N_SB_HEADS = 8
N_FOX_HEADS = 8
CMP_LEN = 32
CMP_STRIDE = 16
SLC_LEN = 64
SLC_TOPK = 8
N_FORCED = 3
WINDOW = 512
N_GROUPS = 4
EXPERTS_PER_GROUP = 4
N_EXPERTS = 16
D_FF = D_MODEL // 8
ALPHA = (2 * DEPTH) ** 0.25
EPS = 1e-5
NEG_BIG = -1e30
SEL_BONUS = 1e6
QK_SCALE = HEAD_DIM ** -0.5

LANES = 128
NSA_QB = 128
SLC_CHUNK = 128
BLK_PER_CHUNK = SLC_CHUNK // SLC_LEN
SLC_PIECES = 4
MOE_EXPERTS_PER_STEP = 2
SB_BLK = 256
FOX_TQ = 512
FOX_TK = 512
EXP_UNDERFLOW = -104.0
BOUND_SLACK = 6.0
SLC_BLK0, WIN_BLK0 = 1536 // 128, 2048 // 128
SB_Q_BLK0, SB_KV_BLK0 = 2560 // 128, 3072 // 128
FOX_Q_BLK0, FOX_KV_BLK0 = 4096 // 128, 4608 // 128
MAIN_W = 5632
VMEM_LIMIT = 56 * 1024 * 1024

_PROJ_SIZES = (1024, 256, 256, 256, 256, 256, 256, 48, 512, 512, 512, 512, 512, 512, 8)
_OFF = np.concatenate([[0], np.cumsum(_PROJ_SIZES)])


def _perm_columns():
    seg = lambda i: np.arange(_OFF[i], _OFF[i + 1])
    nq, ck, cv, sk, sv, wk, wv, ng, sbq, sbk, sbv, fq, fk, fv, ff = [seg(i) for i in range(15)]

    def interleave(k, v, n, swap_odd=False):
        out = []
        for h in range(n):
            kh, vh = k[h * 64:(h + 1) * 64], v[h * 64:(h + 1) * 64]
            out.append(np.concatenate([vh, kh] if (swap_odd and h % 2) else [kh, vh]))
        return np.concatenate(out)

    main = np.concatenate([nq, ck, cv, interleave(sk, sv, 4), interleave(wk, wv, 4), sbq,
                           interleave(sbk, sbv, 8, True), fq, interleave(fk, fv, 8, True)])
    tail = np.concatenate([ng, ff])
    assert main.size == MAIN_W
    return main, tail


_PERM_MAIN, _PERM_TAIL = _perm_columns()


def _perm_mixer_columns():
    def pairs(base, n_heads):
        order = [h for p in range(n_heads // 2) for h in (2 * p + 1, 2 * p)]
        return np.concatenate([base + h * HEAD_DIM + np.arange(HEAD_DIM) for h in order])

    nsa_w = N_NSA_HEADS * HEAD_DIM
    sb_w = N_SB_HEADS * HEAD_DIM
    return np.concatenate([pairs(0, N_NSA_HEADS), pairs(nsa_w, N_SB_HEADS), pairs(nsa_w + sb_w, N_FOX_HEADS)])


_PERM_MIX = _perm_mixer_columns()


def _cparams(sem):
    return pltpu.CompilerParams(dimension_semantics=sem, vmem_limit_bytes=VMEM_LIMIT)


def _dot(a, b):
    return jnp.dot(a, b, preferred_element_type=F32)


def _dot_nt(a, b):
    return lax.dot_general(a, b, (((1,), (1,)), ((), ())), preferred_element_type=F32)


def _split3(x):
    hi = x.astype(BF16)
    r = x - hi.astype(F32)
    mid = r.astype(BF16)
    lo = (r - mid.astype(F32)).astype(BF16)
    return hi, mid, lo


def _dot_split(x, w, parts):
    pieces = _split3(x)[:parts]
    out = _dot(pieces[0], w)
    for p in pieces[1:]:
        out = out + _dot(p, w)
    return out


def _softplus(z):
    return jnp.maximum(z, 0.0) + jnp.log1p(jnp.exp(-jnp.abs(z)))


def _inproj_kernel(x_ref, w_ref, wt_ref, o_ref, ot_ref, xb_ref):
    @pl.when(pl.program_id(1) == 0)
    def _():
        xb = x_ref[...].astype(BF16)
        xb_ref[...] = xb
        ot_ref[...] = _dot(xb, wt_ref[...])

    o_ref[...] = _dot(xb_ref[...], w_ref[...]).astype(BF16)


def _inproj(h, w_main, w_tail):
    t, d = h.shape
    tm, tn = min(1024, t), 512
    return pl.pallas_call(
        _inproj_kernel,
        grid=(t // tm, MAIN_W // tn),
        in_specs=[pl.BlockSpec((tm, d), lambda i, j: (i, 0)),
                  pl.BlockSpec((d, tn), lambda i, j: (0, j)),
                  pl.BlockSpec((d, LANES), lambda i, j: (0, 0))],
        out_specs=[pl.BlockSpec((tm, tn), lambda i, j: (i, j)),
                   pl.BlockSpec((tm, LANES), lambda i, j: (i, 0))],
        out_shape=[jax.ShapeDtypeStruct((t, MAIN_W), BF16), jax.ShapeDtypeStruct((t, LANES), F32)],
        scratch_shapes=[pltpu.VMEM((tm, d), BF16)],
        compiler_params=_cparams(("parallel", "arbitrary")),
        name="inproj",
    )(h, w_main, w_tail)


def _cumsum_kernel(ff_ref, b_ref, c_ref, *, nb):
    x = ff_ref[...] + b_ref[...]
    lf = -_softplus(-x)
    r = lf.shape[0]
    jj = lax.broadcasted_iota(jnp.int32, (LANES, LANES), 0)
    kk = lax.broadcasted_iota(jnp.int32, (LANES, LANES), 1)
    upper = jnp.where(jj <= kk, 1.0, 0.0).astype(BF16)
    ones = jnp.ones((LANES, LANES), BF16)
    within = _dot_split(lf, upper, 3)
    tot = _dot_split(lf, ones, 3)
    rr = lax.broadcasted_iota(jnp.int32, (r, r), 0)
    cc = lax.broadcasted_iota(jnp.int32, (r, r), 1)
    earlier = jnp.where((cc < rr) & (cc // nb == rr // nb), 1.0, 0.0).astype(BF16)
    hi, mid, lo = _split3(tot)
    off = _dot(earlier, hi) + _dot(earlier, mid) + _dot(earlier, lo)
    c_ref[...] = within + off


def _forget_cumsum(ff_rows, bias_rows, nb):
    r = ff_rows.shape[0]
    return pl.pallas_call(
        functools.partial(_cumsum_kernel, nb=nb),
        out_shape=jax.ShapeDtypeStruct((r, LANES), F32),
        compiler_params=pltpu.CompilerParams(vmem_limit_bytes=VMEM_LIMIT),
        name="forget_cumsum",
    )(ff_rows, bias_rows)


def _gelu_tanh(x):
    return 0.5 * x * (1.0 + jnp.tanh(np.sqrt(2.0 / np.pi).astype(np.float32) * (x + 0.044715 * (x * x * x))))


def _compress_kernel(x_ref, pos_ref, wa_ref, wb_ref, w2_ref, o_ref):
    x = x_ref[...].astype(F32)
    n = x.shape[0]
    first = _dot((x + pos_ref[0:1, :]).astype(BF16), wa_ref[...])
    second = _dot((x + pos_ref[1:2, :]).astype(BF16), wb_ref[...])
    pre = first + pltpu.roll(second, n - 1, 0)
    o_ref[...] = _dot(_gelu_tanh(pre).astype(BF16), w2_ref[...]).astype(BF16)


def _compress(x2, pos2, wa, wb, w2):
    n = x2.shape[0]
    return pl.pallas_call(
        _compress_kernel,
        out_shape=jax.ShapeDtypeStruct((n, N_NSA_KV * HEAD_DIM), BF16),
        compiler_params=pltpu.CompilerParams(vmem_limit_bytes=VMEM_LIMIT),
        name="nsa_compress",
    )(x2, pos2, wa, wb, w2)


def _group_queries(q_ref):
    q = q_ref[...]
    low = lax.broadcasted_iota(jnp.int32, (NSA_QB, LANES), 1) < HEAD_DIM
    zero = jnp.zeros((NSA_QB, LANES), BF16)
    out = []
    for j in range(NSA_HPG // 2):
        pair = q[:, j * LANES:(j + 1) * LANES]
        swapped = pltpu.roll(pair.astype(F32), HEAD_DIM, 1).astype(BF16)
        out += [jnp.where(low, pair, zero), jnp.where(low, swapped, zero)]
    return jnp.concatenate(out, axis=0) * QK_SCALE


def _head_sum(p):
    return p[0:NSA_QB] + p[NSA_QB:2 * NSA_QB] + p[2 * NSA_QB:3 * NSA_QB] + p[3 * NSA_QB:4 * NSA_QB]


def _nsa_cmp_kernel(q_ref, kvc_ref, imap_ref, slope_ref, oc_ref, sel_ref, flag_ref, s_ref, *, chunk):
    t0 = pl.program_id(1) * NSA_QB
    rows = NSA_HPG * NSA_QB
    q = _group_queries(q_ref)
    slope = slope_ref[...]
    n_slc = sel_ref.shape[-1]
    trow = lax.broadcasted_iota(jnp.int32, (rows, 1), 0) % NSA_QB
    n_valid = t0 // CMP_STRIDE + (NSA_QB - CMP_LEN) // CMP_STRIDE + 1
    n_steps = (n_valid + chunk - 1) // chunk
    col = lax.broadcasted_iota(jnp.int32, (1, chunk), 1)

    def logits(c, m):
        n0 = pl.multiple_of(c * chunk, chunk)
        end_rel = (n0 + col) * CMP_STRIDE + (CMP_LEN - 1) - t0
        s = _dot_nt(q, kvc_ref[pl.ds(n0, chunk), :]) + slope * end_rel.astype(F32)
        s = jnp.where(end_rel <= trow, s, NEG_BIG)
        s_ref[:, pl.ds(n0, chunk)] = s
        return jnp.maximum(m, jnp.max(s, axis=-1, keepdims=True))

    m = lax.fori_loop(0, n_steps, logits, jnp.full((rows, 1), NEG_BIG, F32))
    m = jnp.where(m > 0.5 * NEG_BIG, m, 0.0)
    lane = lax.broadcasted_iota(jnp.int32, (chunk, LANES), 1)

    def weights(c, carry):
        acc, imp = carry
        n0 = pl.multiple_of(c * chunk, chunk)
        e = jnp.exp(s_ref[:, pl.ds(n0, chunk)] - m).astype(BF16)
        kvc = kvc_ref[pl.ds(n0, chunk), :]
        acc = acc + _dot(e, jnp.where(lane == 0, jnp.ones_like(kvc), kvc))
        return acc, imp + _dot(e, imap_ref[pl.ds(n0, chunk), :])

    acc, imp = lax.fori_loop(0, n_steps, weights,
                             (jnp.zeros((rows, LANES), F32), jnp.zeros((rows, n_slc), F32)))
    l = acc[:, 0:1]
    inv = 1.0 / jnp.where(l > 0.0, l, 1.0)
    oc_ref[...] = (acc * inv).reshape(NSA_HPG, NSA_QB, LANES)
    imp = _head_sum(imp * inv)

    tq = t0 + lax.broadcasted_iota(jnp.int32, (1, NSA_QB), 1)
    jb = lax.broadcasted_iota(jnp.int32, (n_slc, 1), 0)
    jbf = jb.astype(F32)
    cur = tq // SLC_LEN
    forced = (jb == 0) | (jb == cur) | (jb == cur - 1)
    valid = jb * SLC_LEN <= tq
    score = jnp.where(valid & jnp.logical_not(forced), imp.T, -jnp.inf)
    sel = jnp.where(forced, 1.0, 0.0)
    for _ in range(SLC_TOPK - N_FORCED):
        best = jnp.max(score, axis=0, keepdims=True)
        first = jnp.min(jnp.where(score == best, jbf, float(n_slc)), axis=0, keepdims=True)
        hit = jbf == first
        sel = jnp.where(hit, 1.0, sel)
        score = jnp.where(hit, -jnp.inf, score)
    sel = jnp.where(valid, sel, 0.0).T
    sel_ref[...] = sel.astype(BF16)
    any_q = jnp.max(sel, axis=0, keepdims=True)
    jr = lax.broadcasted_iota(jnp.int32, (n_slc, LANES), 0)
    cr = lax.broadcasted_iota(jnp.int32, (n_slc, LANES), 1)
    group = jnp.where(jr // BLK_PER_CHUNK == cr, 1.0, 0.0).astype(BF16)
    per_chunk = _dot(jnp.broadcast_to(any_q, (8, n_slc)).astype(BF16), group)
    flag_ref[...] = jnp.where(per_chunk > 0.5, 1, 0).astype(jnp.int32)


def _nsa_cmp(main, kvc, imap, slopes):
    t = main.shape[0]
    nq = t // NSA_QB
    n = kvc.shape[1]
    n_slc = t // SLC_LEN
    rows = NSA_HPG * NSA_QB
    chunk = min(512, n)
    return pl.pallas_call(
        functools.partial(_nsa_cmp_kernel, chunk=chunk),
        grid=(N_NSA_KV, nq),
        in_specs=[pl.BlockSpec((NSA_QB, NSA_HPG * HEAD_DIM), lambda g, i: (i, g)),
                  pl.BlockSpec((None, n, LANES), lambda g, i: (g, 0, 0)),
                  pl.BlockSpec((n, n_slc), lambda g, i: (0, 0)),
                  pl.BlockSpec((None, rows, 1), lambda g, i: (g, 0, 0))],
        out_specs=[pl.BlockSpec((NSA_HPG, NSA_QB, LANES), lambda g, i: (g, i, 0)),
                   pl.BlockSpec((None, NSA_QB, n_slc), lambda g, i: (g, i, 0)),
                   pl.BlockSpec((None, None, 8, LANES), lambda g, i: (g, i, 0, 0))],
        out_shape=[jax.ShapeDtypeStruct((N_NSA_HEADS, t, LANES), F32),
                   jax.ShapeDtypeStruct((N_NSA_KV, t, n_slc), BF16),
                   jax.ShapeDtypeStruct((N_NSA_KV, nq, 8, LANES), jnp.int32)],
        scratch_shapes=[pltpu.VMEM((rows, n), F32)],
        compiler_params=_cparams(("parallel", "arbitrary")),
        name="nsa_cmp_select",
    )(main, kvc, imap, slopes)


def _nsa_main_kernel(idx_ref, cnt_ref, q_ref, slc_ref, win_ref, sel_ref, oc_ref, gate_ref, slope_ref, o_ref,
                     *, n_chunks):
    g = pl.program_id(0)
    qb = pl.program_id(1)
    nq = pl.num_programs(1)
    t0 = qb * NSA_QB
    rows = NSA_HPG * NSA_QB
    q = _group_queries(q_ref)
    slope = slope_ref[...]
    trow = lax.broadcasted_iota(jnp.int32, (rows, 1), 0) % NSA_QB
    n_slc = sel_ref.shape[-1]
    sel = sel_ref[...]

    def with_ones(kv):
        lane = lax.broadcasted_iota(jnp.int32, kv.shape, 1)
        return jnp.where(lane == 0, jnp.ones_like(kv), kv)

    band = WINDOW + NSA_QB
    start = pl.multiple_of(jnp.maximum(t0 - WINDOW, 0), NSA_QB)
    kw = win_ref[pl.ds(start, band), :]
    krel = (start - t0) + lax.broadcasted_iota(jnp.int32, (1, band), 1)
    dist = trow - krel
    ok = (dist >= 0) & (dist < WINDOW)
    s = jnp.where(ok, _dot_nt(q, kw) + slope * krel.astype(F32), NEG_BIG)
    m = jnp.max(s, axis=-1, keepdims=True)
    e = jnp.where(ok, jnp.exp(s - m), 0.0)
    o_w = _dot(e.astype(BF16), with_ones(kw))
    o_w = o_w / o_w[:, 0:1]

    def picked_keys(block_of_key):
        jr = lax.broadcasted_iota(jnp.int32, (n_slc, block_of_key.shape[1]), 0)
        expand = jnp.where(jr == block_of_key, 1.0, 0.0).astype(BF16)
        return jnp.concatenate([_dot(sel, expand)] * NSA_HPG, axis=0)

    cd = t0 // SLC_CHUNK
    k0 = pl.multiple_of(cd * SLC_CHUNK, SLC_CHUNK)
    kv = slc_ref[pl.ds(k0, SLC_CHUNK), :]
    cr = lax.broadcasted_iota(jnp.int32, (1, SLC_CHUNK), 1)
    krel = (k0 - t0) + cr
    okc = (picked_keys(cd * BLK_PER_CHUNK + cr // SLC_LEN) > 0.5) & (krel <= trow)
    s = jnp.where(okc, _dot_nt(q, kv) + slope * krel.astype(F32), NEG_BIG)
    m = jnp.max(s, axis=-1, keepdims=True)
    p = jnp.where(okc, jnp.exp(s - m), 0.0)
    acc = _dot(p.astype(BF16), with_ones(kv))

    base = (g * nq + qb) * n_chunks
    cnt = cnt_ref[g * nq + qb]
    crp = lax.broadcasted_iota(jnp.int32, (1, SLC_PIECES * SLC_CHUNK), 1)
    piece = crp // SLC_CHUNK
    in_piece = crp - piece * SLC_CHUNK

    def gathered_chunks(i, carry):
        m, acc = carry
        first = SLC_PIECES * i
        chunk_of_key = jnp.zeros_like(crp)
        live = jnp.zeros_like(crp)
        parts = []
        for j in range(SLC_PIECES):
            has = first + j < cnt
            cj = idx_ref[base + jnp.where(has, first + j, first)]
            parts.append(slc_ref[pl.ds(pl.multiple_of(cj * SLC_CHUNK, SLC_CHUNK), SLC_CHUNK), :])
            chunk_of_key = jnp.where(piece == j, cj, chunk_of_key)
            live = jnp.where(piece == j, has.astype(jnp.int32), live)
        kv = jnp.concatenate(parts, axis=0)
        block_of_key = jnp.where(live > 0, chunk_of_key * BLK_PER_CHUNK + in_piece // SLC_LEN, -1)
        krel = (chunk_of_key * SLC_CHUNK + in_piece - t0).astype(F32)
        s = _dot_nt(q, kv) + slope * krel + (picked_keys(block_of_key) - 1.0) * (-NEG_BIG)
        m_new = jnp.maximum(m, jnp.max(s, axis=-1, keepdims=True))
        p = jnp.exp(s - m_new)
        return m_new, jnp.exp(m - m_new) * acc + _dot(p.astype(BF16), with_ones(kv))

    m, acc = lax.fori_loop(0, (cnt + SLC_PIECES - 1) // SLC_PIECES, gathered_chunks, (m, acc))
    o_s = acc / acc[:, 0:1]

    gt = jax.nn.sigmoid(gate_ref[...])
    o_c = oc_ref[...].reshape(rows, LANES)
    o = gt[:, 0:1] * o_c + gt[:, 1:2] * o_s + gt[:, 2:3] * o_w
    low = lax.broadcasted_iota(jnp.int32, (NSA_QB, LANES), 1) < HEAD_DIM
    pairs = []
    for j in range(NSA_HPG // 2):
        even = o[2 * j * NSA_QB:(2 * j + 1) * NSA_QB]
        odd = o[(2 * j + 1) * NSA_QB:(2 * j + 2) * NSA_QB]
        pairs.append(jnp.where(low, pltpu.roll(odd, HEAD_DIM, 1), even))
    o_ref[...] = jnp.concatenate(pairs, axis=1).astype(BF16)


def _nsa_main(idx, cnt, main, sel, oc, gates, slopes):
    t = main.shape[0]
    nq = t // NSA_QB
    n_slc = t // SLC_LEN
    n_chunks = t // SLC_CHUNK
    rows = NSA_HPG * NSA_QB
    width = NSA_HPG * HEAD_DIM
    grid_spec = pltpu.PrefetchScalarGridSpec(
        num_scalar_prefetch=2,
        grid=(N_NSA_KV, nq),
        in_specs=[pl.BlockSpec((NSA_QB, width), lambda g, i, ix, ct: (i, g)),
                  pl.BlockSpec((t, LANES), lambda g, i, ix, ct: (0, SLC_BLK0 + g)),
                  pl.BlockSpec((t, LANES), lambda g, i, ix, ct: (0, WIN_BLK0 + g)),
                  pl.BlockSpec((None, NSA_QB, n_slc), lambda g, i, ix, ct: (g, i, 0)),
                  pl.BlockSpec((NSA_HPG, NSA_QB, LANES), lambda g, i, ix, ct: (g, i, 0)),
                  pl.BlockSpec((None, None, rows, 3), lambda g, i, ix, ct: (g, i, 0, 0)),
                  pl.BlockSpec((None, rows, 1), lambda g, i, ix, ct: (g, 0, 0))],
        out_specs=pl.BlockSpec((NSA_QB, width), lambda g, i, ix, ct: (i, g)),
    )
    return pl.pallas_call(
        functools.partial(_nsa_main_kernel, n_chunks=n_chunks),
        grid_spec=grid_spec,
        out_shape=jax.ShapeDtypeStruct((t, N_NSA_HEADS * HEAD_DIM), BF16),
        compiler_params=_cparams(("parallel", "arbitrary")),
        name="nsa_select_window",
    )(idx, cnt, main, main, main, sel, oc, gates, slopes)


def _pair_queries(q_ref):
    q = q_ref[...]
    low = lax.broadcasted_iota(jnp.int32, q.shape, 1) < HEAD_DIM
    zero = jnp.zeros_like(q)
    return jnp.where(low, q, zero) * QK_SCALE, jnp.where(low, zero, q) * QK_SCALE, low


def _sb_kernel(q_ref, kve_ref, kvo_ref, o_ref):
    qi = pl.program_id(1)
    q_e, q_o, low = _pair_queries(q_ref)
    rr = lax.broadcasted_iota(jnp.int32, (SB_BLK, SB_BLK), 0)
    cc = lax.broadcasted_iota(jnp.int32, (SB_BLK, SB_BLK), 1)
    later = jnp.where(rr > cc, 1.0, 0.0).astype(BF16)
    ones = jnp.ones((SB_BLK, LANES), BF16)

    def head_block(q, kv_ref, kb, mask, run, acc):
        kv = kv_ref[pl.ds(pl.multiple_of(kb * SB_BLK, SB_BLK), SB_BLK), :]
        z = _dot_nt(q, kv)
        sp = _softplus(z)
        log_1mb = -sp if mask is None else jnp.where(mask, -sp, 0.0)
        hi, lo = _split3(log_1mb)[:2]
        within = _dot(hi, later) + _dot(lo, later)
        total = _dot(hi, ones) + _dot(lo, ones)
        a = jnp.exp(z - sp + within + run[:, 0:1])
        if mask is not None:
            a = jnp.where(mask, a, 0.0)
        return run + total, acc + _dot(a.astype(BF16), kv)

    def both(kb, mask, st):
        run_e, acc_e, run_o, acc_o = st
        run_e, acc_e = head_block(q_e, kve_ref, kb, mask, run_e, acc_e)
        run_o, acc_o = head_block(q_o, kvo_ref, kb, mask, run_o, acc_o)
        return run_e, acc_e, run_o, acc_o

    def alive(st):
        return (jnp.max(jnp.maximum(st[0], st[2])) > EXP_UNDERFLOW).astype(jnp.int32)

    zero = jnp.zeros((SB_BLK, LANES), F32)
    st = both(qi, cc < rr, (zero, zero, zero, zero))

    def cond(c):
        return (c[0] >= 0) & (c[1] > 0)

    def body(c):
        st = both(c[0], None, c[2])
        return c[0] - 1, alive(st), st

    _, _, st = lax.while_loop(cond, body, (qi - 1, alive(st), st))
    o_ref[...] = jnp.where(low, st[3], st[1]).astype(BF16)


def _stick_breaking(main):
    t = main.shape[0]
    return pl.pallas_call(
        _sb_kernel,
        grid=(N_SB_HEADS // 2, t // SB_BLK),
        in_specs=[pl.BlockSpec((SB_BLK, LANES), lambda p, i: (i, SB_Q_BLK0 + p)),
                  pl.BlockSpec((t, LANES), lambda p, i: (0, SB_KV_BLK0 + 2 * p)),
                  pl.BlockSpec((t, LANES), lambda p, i: (0, SB_KV_BLK0 + 2 * p + 1))],
        out_specs=pl.BlockSpec((SB_BLK, LANES), lambda p, i: (i, p)),
        out_shape=jax.ShapeDtypeStruct((t, N_SB_HEADS * HEAD_DIM), BF16),
        compiler_params=_cparams(("parallel", "arbitrary")),
        name="stick_breaking",
    )(main, main, main)


def _fox_kernel(cend_ref, q_ref, kve_ref, kvo_ref, c_ref, o_ref, kmax_ref):
    pair = pl.program_id(0)
    qi = pl.program_id(1)
    ratio = FOX_TQ // FOX_TK
    nkb = pl.num_programs(1) * ratio
    t = kve_ref.shape[0]
    scan = 1024

    @pl.when(qi == 0)
    def _():
        def max_norm(ref, keys_low):
            keep = (lax.broadcasted_iota(jnp.int32, (scan, LANES), 1) < HEAD_DIM) == keys_low

            def step(r, best):
                k = ref[pl.ds(pl.multiple_of(r * scan, scan), scan), :].astype(F32)
                k = jnp.where(keep, k, 0.0)
                return jnp.maximum(best, jnp.sum(k * k, axis=-1, keepdims=True))

            best = lax.fori_loop(0, t // scan, step, jnp.zeros((scan, 1), F32))
            return jnp.sqrt(jnp.max(best, axis=0, keepdims=True))

        kmax_ref[0:1, :] = jnp.broadcast_to(max_norm(kve_ref, True), (1, LANES))
        kmax_ref[1:2, :] = jnp.broadcast_to(max_norm(kvo_ref, False), (1, LANES))

    q_e, q_o, low = _pair_queries(q_ref)

    def logit_bound(q, row):
        qf = q.astype(F32)
        return jnp.sqrt(jnp.sum(qf * qf, axis=-1, keepdims=True)) * kmax_ref[row:row + 1, 0:1]

    bound = (logit_bound(q_e, 0), logit_bound(q_o, 1))
    rr = lax.broadcasted_iota(jnp.int32, (FOX_TQ, FOX_TK), 0)
    cc = lax.broadcasted_iota(jnp.int32, (FOX_TQ, FOX_TK), 1)

    key_lane = lax.broadcasted_iota(jnp.int32, (FOX_TK, LANES), 1)
    one_bf = jnp.ones((FOX_TK, LANES), BF16)

    def block_keys(kv_ref, kb):
        return kv_ref[pl.ds(pl.multiple_of(kb * FOX_TK, FOX_TK), FOX_TK), :]

    def scores(q, kv_ref, row, kb):
        k0 = pl.multiple_of(kb * FOX_TK, FOX_TK)
        return _dot_nt(q, block_keys(kv_ref, kb)) - c_ref[row:row + 1, pl.ds(k0, FOX_TK)]

    def consume(s, kv_ref, row, kb, mask, st):
        m, acc = st
        if mask is not None:
            s = jnp.where(mask, s, NEG_BIG)
        m_new = jnp.maximum(m, jnp.max(s, axis=-1, keepdims=True))
        p = jnp.exp(s - m_new)
        if mask is not None:
            p = jnp.where(mask, p, 0.0)
        vals = jnp.where(key_lane == row * HEAD_DIM, one_bf, block_keys(kv_ref, kb))
        return m_new, jnp.exp(m - m_new) * acc + _dot(p.astype(BF16), vals)

    def both_scores(kb):
        return scores(q_e, kve_ref, 0, kb), scores(q_o, kvo_ref, 1, kb)

    def both(s, kb, mask, st):
        return (consume(s[0], kve_ref, 0, kb, mask, st[0]), consume(s[1], kvo_ref, 1, kb, mask, st[1]))

    def needed(kb, st):
        kb = jnp.maximum(kb, 0)
        go = jnp.zeros((), jnp.bool_)
        for row in (0, 1):
            top = jnp.max(bound[row] - st[row][0])
            go = go | (top - cend_ref[(2 * pair + row) * nkb + kb] >= EXP_UNDERFLOW - BOUND_SLACK)
        return go.astype(jnp.int32)

    one = (jnp.full((FOX_TQ, 1), NEG_BIG, F32), jnp.zeros((FOX_TQ, LANES), F32))
    st = (one, one)
    for j in reversed(range(ratio)):
        kb = qi * ratio + j
        st = both(both_scores(kb), kb, rr - cc >= j * FOX_TK, st)

    def cond(c):
        return (c[0] >= 0) & (c[1] > 0)

    def body(c):
        st = both(both_scores(c[0]), c[0], None, c[2])
        return c[0] - 1, needed(c[0] - 1, st), st

    kb0 = qi * ratio - 1
    _, _, st = lax.while_loop(cond, body, (kb0, needed(kb0, st), st))
    (_, acc_e), (_, acc_o) = st
    o_e = acc_e / acc_e[:, 0:1]
    o_o = acc_o / acc_o[:, HEAD_DIM:HEAD_DIM + 1]
    o_ref[...] = jnp.where(low, o_o, o_e).astype(BF16)


def _forgetting(main, c, cend):
    t = main.shape[0]
    grid_spec = pltpu.PrefetchScalarGridSpec(
        num_scalar_prefetch=1,
        grid=(N_FOX_HEADS // 2, t // FOX_TQ),
        in_specs=[pl.BlockSpec((FOX_TQ, LANES), lambda p, i, ce: (i, FOX_Q_BLK0 + p)),
                  pl.BlockSpec((t, LANES), lambda p, i, ce: (0, FOX_KV_BLK0 + 2 * p)),
                  pl.BlockSpec((t, LANES), lambda p, i, ce: (0, FOX_KV_BLK0 + 2 * p + 1)),
                  pl.BlockSpec((None, 2, t), lambda p, i, ce: (p, 0, 0))],
        out_specs=pl.BlockSpec((FOX_TQ, LANES), lambda p, i, ce: (i, p)),
        scratch_shapes=[pltpu.VMEM((8, LANES), F32)],
    )
    return pl.pallas_call(
        _fox_kernel,
        grid_spec=grid_spec,
        out_shape=jax.ShapeDtypeStruct((t, N_FOX_HEADS * HEAD_DIM), BF16),
        compiler_params=_cparams(("parallel", "arbitrary")),
        name="forgetting_attention",
    )(cend, main, main, main, c)


def _layer_norm(x, g, b):
    mu = jnp.mean(x, axis=-1, keepdims=True)
    xc = x - mu
    var = jnp.mean(xc * xc, axis=-1, keepdims=True)
    return xc * lax.rsqrt(var + EPS) * g + b


def _outproj_kernel(yn_ref, ys_ref, yf_ref, h_ref, gn_ref, w_ref, g_ref, b_ref, o_ref):
    gn = gn_ref[...]
    parts = []
    lo = 0
    for y_ref in (yn_ref, ys_ref, yf_ref):
        yp = y_ref[...].astype(F32)
        hi = lo + yp.shape[1]
        ms = jnp.mean(yp * yp, axis=-1, keepdims=True)
        parts.append((yp * lax.rsqrt(ms + EPS) * gn[:, lo:hi]).astype(BF16))
        lo = hi
    mix = _dot(jnp.concatenate(parts, axis=-1), w_ref[...])
    o_ref[...] = _layer_norm(ALPHA * h_ref[...] + mix, g_ref[...], b_ref[...])


def _outproj(y_nsa, y_sb, y_fox, h, gn, w, g, b):
    t, d = h.shape
    tm = min(512, t)
    row = lambda i: (i, 0)
    full = lambda i: (0, 0)
    return pl.pallas_call(
        _outproj_kernel,
        grid=(t // tm,),
        in_specs=[pl.BlockSpec((tm, y_nsa.shape[1]), row), pl.BlockSpec((tm, y_sb.shape[1]), row),
                  pl.BlockSpec((tm, y_fox.shape[1]), row), pl.BlockSpec((tm, d), row), pl.BlockSpec((1, d), full),
                  pl.BlockSpec((d, d), full), pl.BlockSpec((1, d), full), pl.BlockSpec((1, d), full)],
        out_specs=pl.BlockSpec((tm, d), row),
        out_shape=jax.ShapeDtypeStruct((t, d), F32),
        compiler_params=_cparams(("parallel",)),
        name="norm_outproj_ln",
    )(y_nsa, y_sb, y_fox, h, gn, w, g, b)


def _moe_kernel(h_ref, rw_ref, rb_ref, wgu_ref, wd_ref, g_ref, b_ref, o_ref, xb_ref, gate_ref, acc_ref):
    e = pl.program_id(1)

    @pl.when(e == 0)
    def _():
        x = h_ref[...]
        xh, xm, _ = _split3(x)
        xb_ref[...] = xh
        w_hi = rw_ref[0]
        w_lo = rw_ref[1]
        logits = _dot(xh, w_hi) + _dot(xm, w_hi) + _dot(xh, w_lo) + rb_ref[...]
        lane = lax.broadcasted_iota(jnp.int32, logits.shape, 1)
        is_g = lane < N_GROUPS
        gl = jnp.where(is_g, logits, -jnp.inf)
        gmax = jnp.max(gl, axis=-1, keepdims=True)
        g_sel = jnp.min(jnp.where(gl == gmax, lane, LANES), axis=-1, keepdims=True)
        g_w = 1.0 / jnp.sum(jnp.where(is_g, jnp.exp(gl - gmax), 0.0), axis=-1, keepdims=True)
        ex = lane - N_GROUPS
        in_grp = (ex >= 0) & (ex < N_EXPERTS) & (ex // EXPERTS_PER_GROUP == g_sel)
        el = jnp.where(in_grp, logits, -jnp.inf)
        v1 = jnp.max(el, axis=-1, keepdims=True)
        i1 = jnp.min(jnp.where(el == v1, lane, LANES), axis=-1, keepdims=True)
        el2 = jnp.where(lane == i1, -jnp.inf, el)
        v2 = jnp.max(el2, axis=-1, keepdims=True)
        i2 = jnp.min(jnp.where(el2 == v2, lane, LANES), axis=-1, keepdims=True)
        e2 = jnp.exp(v2 - v1)
        w1 = g_w / (1.0 + e2)
        w2 = g_w * e2 / (1.0 + e2)
        gate_ref[...] = jnp.where(lane == i1, w1, 0.0) + jnp.where(lane == i2, w2, 0.0)
        acc_ref[...] = jnp.zeros_like(acc_ref)

    xb = xb_ref[...]
    lane = lax.broadcasted_iota(jnp.int32, gate_ref.shape, 1)
    update = jnp.zeros(acc_ref.shape, F32)
    for j in range(MOE_EXPERTS_PER_STEP):
        gu = _dot(xb, wgu_ref[j])
        expert_lane = e * MOE_EXPERTS_PER_STEP + j + N_GROUPS
        gcol = jnp.sum(jnp.where(lane == expert_lane, gate_ref[...], 0.0), axis=-1, keepdims=True)
        hid = jax.nn.silu(gu[:, :D_FF]) * gu[:, D_FF:]
        update = update + _dot((hid * gcol).astype(BF16), wd_ref[j])
    acc_ref[...] += update

    @pl.when(e == N_EXPERTS // MOE_EXPERTS_PER_STEP - 1)
    def _():
        o_ref[...] = _layer_norm(ALPHA * h_ref[...] + acc_ref[...], g_ref[...], b_ref[...])


def _moe(h, rw, rb, wgu, wd, g, b):
    t, d = h.shape
    tm = min(512, t)
    return pl.pallas_call(
        _moe_kernel,
        grid=(t // tm, N_EXPERTS // MOE_EXPERTS_PER_STEP),
        in_specs=[pl.BlockSpec((tm, d), lambda i, e: (i, 0)),
                  pl.BlockSpec((2, d, LANES), lambda i, e: (0, 0, 0)),
                  pl.BlockSpec((1, LANES), lambda i, e: (0, 0)),
                  pl.BlockSpec((MOE_EXPERTS_PER_STEP, d, 2 * D_FF), lambda i, e: (e, 0, 0)),
                  pl.BlockSpec((MOE_EXPERTS_PER_STEP, D_FF, d), lambda i, e: (e, 0, 0)),
                  pl.BlockSpec((1, d), lambda i, e: (0, 0)),
                  pl.BlockSpec((1, d), lambda i, e: (0, 0))],
        out_specs=pl.BlockSpec((tm, d), lambda i, e: (i, 0)),
        out_shape=jax.ShapeDtypeStruct((t, d), F32),
        scratch_shapes=[pltpu.VMEM((tm, d), BF16), pltpu.VMEM((tm, LANES), F32), pltpu.VMEM((tm, d), F32)],
        compiler_params=_cparams(("parallel", "arbitrary")),
        name="moe_ln",
    )(h, rw, rb, wgu, wd, g, b)


def _cmp_to_slc_matrix(n_cmp_pad, n_slc):
    r, c = SLC_LEN // CMP_STRIDE, CMP_LEN // CMP_STRIDE
    m = np.zeros((n_cmp_pad, n_slc), np.float32)
    for j in range(n_slc):
        for a in range(r):
            for bb in range(c):
                i = r * j + a + bb
                if i < n_cmp_pad - 1:
                    m[i, j] += 1.0
    return m


def _compress_weights(pos, w1, w2):
    eye = jnp.eye(N_NSA_KV, dtype=F32)
    half = CMP_LEN // 2
    cols = N_NSA_KV * HEAD_DIM

    def expand(w):
        return jnp.einsum('lde,gh->lgdhe', w, eye).reshape(half * cols, cols).astype(BF16)

    def tile_pos(p):
        return jnp.broadcast_to(p[:, None, :], (half, N_NSA_KV, HEAD_DIM)).reshape(1, half * cols)

    pos2 = jnp.concatenate([tile_pos(pos[:half]), tile_pos(pos[half:])], axis=0)
    w2bd = jnp.einsum('ef,gh->gehf', w2, eye).reshape(cols, cols).astype(BF16)
    return pos2, expand(w1[:half]), expand(w1[half:]), w2bd


def _mixer(h, w_in, cmp_pos_k, cmp_w1_k, cmp_w2_k, cmp_pos_v, cmp_w1_v, cmp_w2_v, fox_forget_bias,
           norm_nsa, norm_sb, norm_fox, w_out, ln_g, ln_b):
    t = h.shape[0]
    nq = t // NSA_QB
    w_main = w_in[:, _PERM_MAIN].astype(BF16)
    w_tail = jnp.pad(w_in[:, _PERM_TAIL], ((0, 0), (0, LANES - _PERM_TAIL.size))).astype(BF16)
    main, tail = _inproj(h, w_main, w_tail)

    n_rows = t // CMP_STRIDE
    kc = _compress(main[:, 1024:1280].reshape(n_rows, -1), *_compress_weights(cmp_pos_k, cmp_w1_k, cmp_w2_k))
    vc = _compress(main[:, 1280:1536].reshape(n_rows, -1), *_compress_weights(cmp_pos_v, cmp_w1_v, cmp_w2_v))
    kvc = jnp.concatenate([kc.reshape(n_rows, N_NSA_KV, HEAD_DIM), vc.reshape(n_rows, N_NSA_KV, HEAD_DIM)],
                          axis=-1).transpose(1, 0, 2)
    slopes = 2.0 **(-8.0 * jnp.arange(1, N_NSA_HEADS + 1, dtype=F32) / N_NSA_HEADS)
    slopes = jnp.repeat(slopes.reshape(N_NSA_KV, NSA_HPG), NSA_QB, axis=1)[..., None]
    imap = jnp.asarray(_cmp_to_slc_matrix(n_rows, t // SLC_LEN), BF16)
    oc, sel, flags = _nsa_cmp(main, kvc, imap, slopes)
    n_chunks = t // SLC_CHUNK
    diag = (jnp.arange(nq) * NSA_QB) // SLC_CHUNK
    active = flags[:, :, 0, :n_chunks] * (jnp.arange(n_chunks)[None, None, :] != diag[None, :, None])
    idx = jnp.argsort(1 - active, axis=-1, stable=True).astype(jnp.int32).reshape(-1)
    cnt = jnp.sum(active, axis=-1).astype(jnp.int32).reshape(-1)
    gates = tail[:, :48].reshape(nq, NSA_QB, N_NSA_KV, NSA_HPG, 3).transpose(2, 0, 3, 1, 4)
    gates = gates.reshape(N_NSA_KV, nq, NSA_HPG * NSA_QB, 3)
    o_nsa = _nsa_main(idx, cnt, main, sel, oc, gates, slopes)

    o_sb = _stick_breaking(main)

    nb = t // LANES
    ff_rows = tail[:, 48:56].T.reshape(N_FOX_HEADS * nb, LANES)
    bias_rows = jnp.repeat(fox_forget_bias, nb)[:, None]
    c = _forget_cumsum(ff_rows, bias_rows, nb).reshape(N_FOX_HEADS, t)
    cend = c[:, FOX_TK - 1::FOX_TK].reshape(-1)
    o_fox = _forgetting(main, c.reshape(N_FOX_HEADS // 2, 2, t), cend)

    gn = jnp.concatenate([norm_nsa, norm_sb, norm_fox])[_PERM_MIX][None, :]
    return _outproj(o_nsa, o_sb, o_fox, h, gn, w_out[_PERM_MIX].astype(BF16), ln_g[None, :], ln_b[None, :])


def _ffn(h, rg_w, rg_b, re_w, re_b, w_gate, w_up, w_down, ln_g, ln_b):
    rw = jnp.pad(jnp.concatenate([rg_w, re_w], axis=1), ((0, 0), (0, LANES - N_GROUPS - N_EXPERTS)))
    rw_hi = rw.astype(BF16)
    rw_lo = (rw - rw_hi.astype(F32)).astype(BF16)
    rb = jnp.pad(jnp.concatenate([rg_b, re_b]), (0, LANES - N_GROUPS - N_EXPERTS))[None, :]
    wgu = jnp.concatenate([w_gate, w_up], axis=-1).astype(BF16)
    return _moe(h, jnp.stack([rw_hi, rw_lo]), rb, wgu, w_down.astype(BF16), ln_g[None, :], ln_b[None, :])


def kernel(x, w_in, cmp_pos_k, cmp_w1_k, cmp_w2_k, cmp_pos_v, cmp_w1_v, cmp_w2_v, fox_forget_bias, norm_nsa, norm_sb, norm_fox, w_out, ln1_g, ln1_b, router_group_w, router_group_b, router_expert_w, router_expert_b, expert_w_gate, expert_w_up, expert_w_down, ln2_g, ln2_b):
    b, t, d = x.shape
    assert b == 1 and d == D_MODEL and t % 1024 == 0
    h = x.reshape(t, d)
    for l in range(DEPTH):
        h = _mixer(h, w_in[l], cmp_pos_k[l], cmp_w1_k[l], cmp_w2_k[l], cmp_pos_v[l], cmp_w1_v[l], cmp_w2_v[l],
                   fox_forget_bias[l], norm_nsa[l], norm_sb[l], norm_fox[l], w_out[l], ln1_g[l], ln1_b[l])
        h = _ffn(h, router_group_w[l], router_group_b[l], router_expert_w[l], router_expert_b[l],
                 expert_w_gate[l], expert_w_up[l], expert_w_down[l], ln2_g[l], ln2_b[l])
    return h.reshape(b, t, d)
```

```python
import functools

import numpy as np
import jax
import jax.numpy as jnp
from jax import lax
from jax.experimental import pallas as pl
from jax.experimental.pallas import tpu as pltpu

F32 = jnp.float32
BF16 = jnp.bfloat16

D_MODEL = 2048
DEPTH = 2
HEAD_DIM = 64
N_NSA_HEADS = 16
N_NSA_KV = 4
NSA_HPG = 4
N_SB_HEADS = 8
N_FOX_HEADS = 8
CMP_LEN = 32
CMP_STRIDE = 16
SLC_LEN = 64
SLC_TOPK = 8
N_FORCED = 3
WINDOW = 512
N_GROUPS = 4
EXPERTS_PER_GROUP = 4
N_EXPERTS = 16
D_FF = D_MODEL // 8
ALPHA = (2 * DEPTH) ** 0.25
EPS = 1e-5
NEG_BIG = -1e30
SEL_BONUS = 1e6
QK_SCALE = HEAD_DIM ** -0.5

LANES = 128
NSA_QB = 128
SLC_CHUNK = 128
BLK_PER_CHUNK = SLC_CHUNK // SLC_LEN
SLC_PIECES = 4
MOE_EXPERTS_PER_STEP = 2
SB_BLK = 256
FOX_TQ = 512
FOX_TK = 512
EXP_UNDERFLOW = -104.0
BOUND_SLACK = 6.0
SLC_BLK0, WIN_BLK0 = 1536 // 128, 2048 // 128
SB_Q_BLK0, SB_KV_BLK0 = 2560 // 128, 3072 // 128
FOX_Q_BLK0, FOX_KV_BLK0 = 4096 // 128, 4608 // 128
MAIN_W = 5632
VMEM_LIMIT = 56 * 1024 * 1024

_PROJ_SIZES = (1024, 256, 256, 256, 256, 256, 256, 48, 512, 512, 512, 512, 512, 512, 8)
_OFF = np.concatenate([[0], np.cumsum(_PROJ_SIZES)])


def _perm_columns():
    seg = lambda i: np.arange(_OFF[i], _OFF[i + 1])
    nq, ck, cv, sk, sv, wk, wv, ng, sbq, sbk, sbv, fq, fk, fv, ff = [seg(i) for i in range(15)]

    def interleave(k, v, n, swap_odd=False):
        out = []
        for h in range(n):
            kh, vh = k[h * 64:(h + 1) * 64], v[h * 64:(h + 1) * 64]
            out.append(np.concatenate([vh, kh] if (swap_odd and h % 2) else [kh, vh]))
        return np.concatenate(out)

    main = np.concatenate([nq, ck, cv, interleave(sk, sv, 4), interleave(wk, wv, 4), sbq,
                           interleave(sbk, sbv, 8, True), fq, interleave(fk, fv, 8, True)])
    tail = np.concatenate([ng, ff])
    assert main.size == MAIN_W
    return main, tail


_PERM_MAIN, _PERM_TAIL = _perm_columns()


def _perm_mixer_columns():
    def pairs(base, n_heads):
        order = [h for p in range(n_heads // 2) for h in (2 * p + 1, 2 * p)]
        return np.concatenate([base + h * HEAD_DIM + np.arange(HEAD_DIM) for h in order])

    nsa_w = N_NSA_HEADS * HEAD_DIM
    sb_w = N_SB_HEADS * HEAD_DIM
    return np.concatenate([pairs(0, N_NSA_HEADS), pairs(nsa_w, N_SB_HEADS), pairs(nsa_w + sb_w, N_FOX_HEADS)])


_PERM_MIX = _perm_mixer_columns()


def _cparams(sem):
    return pltpu.CompilerParams(dimension_semantics=sem, vmem_limit_bytes=VMEM_LIMIT)


def _dot(a, b):
    return jnp.dot(a, b, preferred_element_type=F32)


def _dot_nt(a, b):
    return lax.dot_general(a, b, (((1,), (1,)), ((), ())), preferred_element_type=F32)


def _split3(x):
    hi = x.astype(BF16)
    r = x - hi.astype(F32)
    mid = r.astype(BF16)
    lo = (r - mid.astype(F32)).astype(BF16)
    return hi, mid, lo


def _dot_split(x, w, parts):
    pieces = _split3(x)[:parts]
    out = _dot(pieces[0], w)
    for p in pieces[1:]:
        out = out + _dot(p, w)
    return out


def _softplus(z):
    return jnp.maximum(z, 0.0) + jnp.log1p(jnp.exp(-jnp.abs(z)))


def _inproj_kernel(x_ref, w_ref, wt_ref, o_ref, ot_ref, xb_ref):
    @pl.when(pl.program_id(1) == 0)
    def _():
        xb = x_ref[...].astype(BF16)
        xb_ref[...] = xb
        ot_ref[...] = _dot(xb, wt_ref[...])

    o_ref[...] = _dot(xb_ref[...], w_ref[...]).astype(BF16)


def _inproj(h, w_main, w_tail):
    t, d = h.shape
    tm, tn = min(1024, t), MAIN_W // 4
    return pl.pallas_call(
        _inproj_kernel,
        grid=(t // tm, MAIN_W // tn),
        in_specs=[pl.BlockSpec((tm, d), lambda i, j: (i, 0)),
                  pl.BlockSpec((d, tn), lambda i, j: (0, j)),
                  pl.BlockSpec((d, LANES), lambda i, j: (0, 0))],
        out_specs=[pl.BlockSpec((tm, tn), lambda i, j: (i, j)),
                   pl.BlockSpec((tm, LANES), lambda i, j: (i, 0))],
        out_shape=[jax.ShapeDtypeStruct((t, MAIN_W), BF16), jax.ShapeDtypeStruct((t, LANES), F32)],
        scratch_shapes=[pltpu.VMEM((tm, d), BF16)],
        compiler_params=_cparams(("parallel", "arbitrary")),
        name="inproj",
    )(h, w_main, w_tail)


def _cumsum_kernel(ff_ref, b_ref, c_ref, *, nb):
    x = ff_ref[...] + b_ref[...]
    lf = -_softplus(-x)
    r = lf.shape[0]
    jj = lax.broadcasted_iota(jnp.int32, (LANES, LANES), 0)
    kk = lax.broadcasted_iota(jnp.int32, (LANES, LANES), 1)
    upper = jnp.where(jj <= kk, 1.0, 0.0).astype(BF16)
    ones = jnp.ones((LANES, LANES), BF16)
    within = _dot_split(lf, upper, 3)
    tot = _dot_split(lf, ones, 3)
    rr = lax.broadcasted_iota(jnp.int32, (r, r), 0)
    cc = lax.broadcasted_iota(jnp.int32, (r, r), 1)
    earlier = jnp.where((cc < rr) & (cc // nb == rr // nb), 1.0, 0.0).astype(BF16)
    hi, mid, lo = _split3(tot)
    off = _dot(earlier, hi) + _dot(earlier, mid) + _dot(earlier, lo)
    c_ref[...] = within + off


def _forget_cumsum(ff_rows, bias_rows, nb):
    r = ff_rows.shape[0]
    return pl.pallas_call(
        functools.partial(_cumsum_kernel, nb=nb),
        out_shape=jax.ShapeDtypeStruct((r, LANES), F32),
        compiler_params=pltpu.CompilerParams(vmem_limit_bytes=VMEM_LIMIT),
        name="forget_cumsum",
    )(ff_rows, bias_rows)


def _gelu_tanh(x):
    return 0.5 * x * (1.0 + jnp.tanh(np.sqrt(2.0 / np.pi).astype(np.float32) * (x + 0.044715 * (x * x * x))))


def _compress_kernel(x_ref, pos_ref, wa_ref, wb_ref, w2_ref, o_ref):
    x = x_ref[...].astype(F32)
    n = x.shape[0]
    first = _dot((x + pos_ref[0:1, :]).astype(BF16), wa_ref[...])
    second = _dot((x + pos_ref[1:2, :]).astype(BF16), wb_ref[...])
    pre = first + pltpu.roll(second, n - 1, 0)
    o_ref[...] = _dot(_gelu_tanh(pre).astype(BF16), w2_ref[...]).astype(BF16)


def _compress(x2, pos2, wa, wb, w2):
    n = x2.shape[0]
    return pl.pallas_call(
        _compress_kernel,
        out_shape=jax.ShapeDtypeStruct((n, N_NSA_KV * HEAD_DIM), BF16),
        compiler_params=pltpu.CompilerParams(vmem_limit_bytes=VMEM_LIMIT),
        name="nsa_compress",
    )(x2, pos2, wa, wb, w2)


def _group_queries(q_ref):
    q = q_ref[...]
    low = lax.broadcasted_iota(jnp.int32, (NSA_QB, LANES), 1) < HEAD_DIM
    zero = jnp.zeros((NSA_QB, LANES), BF16)
    out = []
    for j in range(NSA_HPG // 2):
        pair = q[:, j * LANES:(j + 1) * LANES]
        swapped = pltpu.roll(pair.astype(F32), HEAD_DIM, 1).astype(BF16)
        out += [jnp.where(low, pair, zero), jnp.where(low, swapped, zero)]
    return jnp.concatenate(out, axis=0) * QK_SCALE


def _head_sum(p):
    return p[0:NSA_QB] + p[NSA_QB:2 * NSA_QB] + p[2 * NSA_QB:3 * NSA_QB] + p[3 * NSA_QB:4 * NSA_QB]


def _nsa_cmp_kernel(q_ref, kvc_ref, imap_ref, slope_ref, oc_ref, sel_ref, flag_ref, s_ref, *, chunk):
    t0 = pl.program_id(1) * NSA_QB
    rows = NSA_HPG * NSA_QB
    q = _group_queries(q_ref)
    slope = slope_ref[...]
    n_slc = sel_ref.shape[-1]
    trow = lax.broadcasted_iota(jnp.int32, (rows, 1), 0) % NSA_QB
    n_valid = t0 // CMP_STRIDE + (NSA_QB - CMP_LEN) // CMP_STRIDE + 1
    n_steps = (n_valid + chunk - 1) // chunk
    col = lax.broadcasted_iota(jnp.int32, (1, chunk), 1)

    def logits(c, m):
        n0 = pl.multiple_of(c * chunk, chunk)
        end_rel = (n0 + col) * CMP_STRIDE + (CMP_LEN - 1) - t0
        s = _dot_nt(q, kvc_ref[pl.ds(n0, chunk), :]) + slope * end_rel.astype(F32)
        s = jnp.where(end_rel <= trow, s, NEG_BIG)
        s_ref[:, pl.ds(n0, chunk)] = s
        return jnp.maximum(m, jnp.max(s, axis=-1, keepdims=True))

    m = lax.fori_loop(0, n_steps, logits, jnp.full((rows, 1), NEG_BIG, F32))
    m = jnp.where(m > 0.5 * NEG_BIG, m, 0.0)
    lane = lax.broadcasted_iota(jnp.int32, (chunk, LANES), 1)

    def weights(c, carry):
        acc, imp = carry
        n0 = pl.multiple_of(c * chunk, chunk)
        e = jnp.exp(s_ref[:, pl.ds(n0, chunk)] - m).astype(BF16)
        kvc = kvc_ref[pl.ds(n0, chunk), :]
        acc = acc + _dot(e, jnp.where(lane == 0, jnp.ones_like(kvc), kvc))
        return acc, imp + _dot(e, imap_ref[pl.ds(n0, chunk), :])

    acc, imp = lax.fori_loop(0, n_steps, weights,
                             (jnp.zeros((rows, LANES), F32), jnp.zeros((rows, n_slc), F32)))
    l = acc[:, 0:1]
    inv = 1.0 / jnp.where(l > 0.0, l, 1.0)
    oc_ref[...] = (acc * inv).reshape(NSA_HPG, NSA_QB, LANES)
    imp = _head_sum(imp * inv)

    tq = t0 + lax.broadcasted_iota(jnp.int32, (1, NSA_QB), 1)
    jb = lax.broadcasted_iota(jnp.int32, (n_slc, 1), 0)
    jbf = jb.astype(F32)
    cur = tq // SLC_LEN
    forced = (jb == 0) | (jb == cur) | (jb == cur - 1)
    valid = jb * SLC_LEN <= tq
    score = jnp.where(valid & jnp.logical_not(forced), imp.T, -jnp.inf)
    sel = jnp.where(forced, 1.0, 0.0)
    for _ in range(SLC_TOPK - N_FORCED):
        best = jnp.max(score, axis=0, keepdims=True)
        first = jnp.min(jnp.where(score == best, jbf, float(n_slc)), axis=0, keepdims=True)
        hit = jbf == first
        sel = jnp.where(hit, 1.0, sel)
        score = jnp.where(hit, -jnp.inf, score)
    sel = jnp.where(valid, sel, 0.0).T
    sel_ref[...] = sel.astype(BF16)
    any_q = jnp.max(sel, axis=0, keepdims=True)
    jr = lax.broadcasted_iota(jnp.int32, (n_slc, LANES), 0)
    cr = lax.broadcasted_iota(jnp.int32, (n_slc, LANES), 1)
    group = jnp.where(jr // BLK_PER_CHUNK == cr, 1.0, 0.0).astype(BF16)
    per_chunk = _dot(jnp.broadcast_to(any_q, (8, n_slc)).astype(BF16), group)
    flag_ref[...] = jnp.where(per_chunk > 0.5, 1, 0).astype(jnp.int32)


def _nsa_cmp(main, kvc, imap, slopes):
    t = main.shape[0]
    nq = t // NSA_QB
    n = kvc.shape[1]
    n_slc = t // SLC_LEN
    rows = NSA_HPG * NSA_QB
    chunk = min(512, n)
    return pl.pallas_call(
        functools.partial(_nsa_cmp_kernel, chunk=chunk),
        grid=(N_NSA_KV, nq),
        in_specs=[pl.BlockSpec((NSA_QB, NSA_HPG * HEAD_DIM), lambda g, i: (i, g)),
                  pl.BlockSpec((None, n, LANES), lambda g, i: (g, 0, 0)),
                  pl.BlockSpec((n, n_slc), lambda g, i: (0, 0)),
                  pl.BlockSpec((None, rows, 1), lambda g, i: (g, 0, 0))],
        out_specs=[pl.BlockSpec((NSA_HPG, NSA_QB, LANES), lambda g, i: (g, i, 0)),
                   pl.BlockSpec((None, NSA_QB, n_slc), lambda g, i: (g, i, 0)),
                   pl.BlockSpec((None, None, 8, LANES), lambda g, i: (g, i, 0, 0))],
        out_shape=[jax.ShapeDtypeStruct((N_NSA_HEADS, t, LANES), F32),
                   jax.ShapeDtypeStruct((N_NSA_KV, t, n_slc), BF16),
                   jax.ShapeDtypeStruct((N_NSA_KV, nq, 8, LANES), jnp.int32)],
        scratch_shapes=[pltpu.VMEM((rows, n), F32)],
        compiler_params=_cparams(("parallel", "arbitrary")),
        name="nsa_cmp_select",
    )(main, kvc, imap, slopes)


def _nsa_main_kernel(idx_ref, cnt_ref, q_ref, slc_ref, win_ref, sel_ref, oc_ref, gate_ref, slope_ref, o_ref,
                     *, n_chunks):
    g = pl.program_id(0)
    qb = pl.program_id(1)
    nq = pl.num_programs(1)
    t0 = qb * NSA_QB
    rows = NSA_HPG * NSA_QB
    q = _group_queries(q_ref)
    slope = slope_ref[...]
    trow = lax.broadcasted_iota(jnp.int32, (rows, 1), 0) % NSA_QB
    n_slc = sel_ref.shape[-1]
    sel = sel_ref[...]

    def with_ones(kv):
        lane = lax.broadcasted_iota(jnp.int32, kv.shape, 1)
        return jnp.where(lane == 0, jnp.ones_like(kv), kv)

    band = WINDOW + NSA_QB
    start = pl.multiple_of(jnp.maximum(t0 - WINDOW, 0), NSA_QB)
    kw = win_ref[pl.ds(start, band), :]
    krel = (start - t0) + lax.broadcasted_iota(jnp.int32, (1, band), 1)
    dist = trow - krel
    ok = (dist >= 0) & (dist < WINDOW)
    s = jnp.where(ok, _dot_nt(q, kw) + slope * krel.astype(F32), NEG_BIG)
    m = jnp.max(s, axis=-1, keepdims=True)
    e = jnp.where(ok, jnp.exp(s - m), 0.0)
    o_w = _dot(e.astype(BF16), with_ones(kw))
    o_w = o_w / o_w[:, 0:1]

    def picked_keys(block_of_key):
        jr = lax.broadcasted_iota(jnp.int32, (n_slc, block_of_key.shape[1]), 0)
        expand = jnp.where(jr == block_of_key, 1.0, 0.0).astype(BF16)
        return jnp.concatenate([_dot(sel, expand)] * NSA_HPG, axis=0)

    cd = t0 // SLC_CHUNK
    k0 = pl.multiple_of(cd * SLC_CHUNK, SLC_CHUNK)
    kv = slc_ref[pl.ds(k0, SLC_CHUNK), :]
    cr = lax.broadcasted_iota(jnp.int32, (1, SLC_CHUNK), 1)
    krel = (k0 - t0) + cr
    okc = (picked_keys(cd * BLK_PER_CHUNK + cr // SLC_LEN) > 0.5) & (krel <= trow)
    s = jnp.where(okc, _dot_nt(q, kv) + slope * krel.astype(F32), NEG_BIG)
    m = jnp.max(s, axis=-1, keepdims=True)
    p = jnp.where(okc, jnp.exp(s - m), 0.0)
    acc = _dot(p.astype(BF16), with_ones(kv))

    base = (g * nq + qb) * n_chunks
    cnt = cnt_ref[g * nq + qb]
    crp = lax.broadcasted_iota(jnp.int32, (1, SLC_PIECES * SLC_CHUNK), 1)
    piece = crp // SLC_CHUNK
    in_piece = crp - piece * SLC_CHUNK

    def gathered_chunks(i, carry):
        m, acc = carry
        first = SLC_PIECES * i
        chunk_of_key = jnp.zeros_like(crp)
        live = jnp.zeros_like(crp)
        parts = []
        for j in range(SLC_PIECES):
            has = first + j < cnt
            cj = idx_ref[base + jnp.where(has, first + j, first)]
            parts.append(slc_ref[pl.ds(pl.multiple_of(cj * SLC_CHUNK, SLC_CHUNK), SLC_CHUNK), :])
            chunk_of_key = jnp.where(piece == j, cj, chunk_of_key)
            live = jnp.where(piece == j, has.astype(jnp.int32), live)
        kv = jnp.concatenate(parts, axis=0)
        block_of_key = jnp.where(live > 0, chunk_of_key * BLK_PER_CHUNK + in_piece // SLC_LEN, -1)
        krel = (chunk_of_key * SLC_CHUNK + in_piece - t0).astype(F32)
        s = _dot_nt(q, kv) + slope * krel + (picked_keys(block_of_key) - 1.0) * (-NEG_BIG)
        m_new = jnp.maximum(m, jnp.max(s, axis=-1, keepdims=True))
        p = jnp.exp(s - m_new)
        return m_new, jnp.exp(m - m_new) * acc + _dot(p.astype(BF16), with_ones(kv))

    m, acc = lax.fori_loop(0, (cnt + SLC_PIECES - 1) // SLC_PIECES, gathered_chunks, (m, acc))
    o_s = acc / acc[:, 0:1]

    gt = jax.nn.sigmoid(gate_ref[...])
    o_c = oc_ref[...].reshape(rows, LANES)
    o = gt[:, 0:1] * o_c + gt[:, 1:2] * o_s + gt[:, 2:3] * o_w
    low = lax.broadcasted_iota(jnp.int32, (NSA_QB, LANES), 1) < HEAD_DIM
    pairs = []
    for j in range(NSA_HPG // 2):
        even = o[2 * j * NSA_QB:(2 * j + 1) * NSA_QB]
        odd = o[(2 * j + 1) * NSA_QB:(2 * j + 2) * NSA_QB]
        pairs.append(jnp.where(low, pltpu.roll(odd, HEAD_DIM, 1), even))
    o_ref[...] = jnp.concatenate(pairs, axis=1).astype(BF16)


def _nsa_main(idx, cnt, main, sel, oc, gates, slopes):
    t = main.shape[0]
    nq = t // NSA_QB
    n_slc = t // SLC_LEN
    n_chunks = t // SLC_CHUNK
    rows = NSA_HPG * NSA_QB
    width = NSA_HPG * HEAD_DIM
    grid_spec = pltpu.PrefetchScalarGridSpec(
        num_scalar_prefetch=2,
        grid=(N_NSA_KV, nq),
        in_specs=[pl.BlockSpec((NSA_QB, width), lambda g, i, ix, ct: (i, g)),
                  pl.BlockSpec((t, LANES), lambda g, i, ix, ct: (0, SLC_BLK0 + g)),
                  pl.BlockSpec((t, LANES), lambda g, i, ix, ct: (0, WIN_BLK0 + g)),
                  pl.BlockSpec((None, NSA_QB, n_slc), lambda g, i, ix, ct: (g, i, 0)),
                  pl.BlockSpec((NSA_HPG, NSA_QB, LANES), lambda g, i, ix, ct: (g, i, 0)),
                  pl.BlockSpec((None, None, rows, 3), lambda g, i, ix, ct: (g, i, 0, 0)),
                  pl.BlockSpec((None, rows, 1), lambda g, i, ix, ct: (g, 0, 0))],
        out_specs=pl.BlockSpec((NSA_QB, width), lambda g, i, ix, ct: (i, g)),
    )
    return pl.pallas_call(
        functools.partial(_nsa_main_kernel, n_chunks=n_chunks),
        grid_spec=grid_spec,
        out_shape=jax.ShapeDtypeStruct((t, N_NSA_HEADS * HEAD_DIM), BF16),
        compiler_params=_cparams(("parallel", "arbitrary")),
        name="nsa_select_window",
    )(idx, cnt, main, main, main, sel, oc, gates, slopes)


def _pair_queries(q_ref):
    q = q_ref[...]
    low = lax.broadcasted_iota(jnp.int32, q.shape, 1) < HEAD_DIM
    zero = jnp.zeros_like(q)
    return jnp.where(low, q, zero) * QK_SCALE, jnp.where(low, zero, q) * QK_SCALE, low


def _sb_kernel(q_ref, kve_ref, kvo_ref, o_ref):
    qi = pl.program_id(1)
    q_e, q_o, low = _pair_queries(q_ref)
    rr = lax.broadcasted_iota(jnp.int32, (SB_BLK, SB_BLK), 0)
    cc = lax.broadcasted_iota(jnp.int32, (SB_BLK, SB_BLK), 1)
    later = jnp.where(rr > cc, 1.0, 0.0).astype(BF16)
    ones = jnp.ones((SB_BLK, LANES), BF16)

    def head_block(q, kv_ref, kb, mask, run, acc):
        kv = kv_ref[pl.ds(pl.multiple_of(kb * SB_BLK, SB_BLK), SB_BLK), :]
        z = _dot_nt(q, kv)
        sp = _softplus(z)
        log_1mb = -sp if mask is None else jnp.where(mask, -sp, 0.0)
        hi, lo = _split3(log_1mb)[:2]
        within = _dot(hi, later) + _dot(lo, later)
        total = _dot(hi, ones) + _dot(lo, ones)
        a = jnp.exp(z - sp + within + run[:, 0:1])
        if mask is not None:
            a = jnp.where(mask, a, 0.0)
        return run + total, acc + _dot(a.astype(BF16), kv)

    def both(kb, mask, st):
        run_e, acc_e, run_o, acc_o = st
        run_e, acc_e = head_block(q_e, kve_ref, kb, mask, run_e, acc_e)
        run_o, acc_o = head_block(q_o, kvo_ref, kb, mask, run_o, acc_o)
        return run_e, acc_e, run_o, acc_o

    def alive(st):
        return (jnp.max(jnp.maximum(st[0], st[2])) > EXP_UNDERFLOW).astype(jnp.int32)

    zero = jnp.zeros((SB_BLK, LANES), F32)
    st = both(qi, cc < rr, (zero, zero, zero, zero))

    def cond(c):
        return (c[0] >= 0) & (c[1] > 0)

    def body(c):
        st = both(c[0], None, c[2])
        return c[0] - 1, alive(st), st

    _, _, st = lax.while_loop(cond, body, (qi - 1, alive(st), st))
    o_ref[...] = jnp.where(low, st[3], st[1]).astype(BF16)


def _stick_breaking(main):
    t = main.shape[0]
    return pl.pallas_call(
        _sb_kernel,
        grid=(N_SB_HEADS // 2, t // SB_BLK),
        in_specs=[pl.BlockSpec((SB_BLK, LANES), lambda p, i: (i, SB_Q_BLK0 + p)),
                  pl.BlockSpec((t, LANES), lambda p, i: (0, SB_KV_BLK0 + 2 * p)),
                  pl.BlockSpec((t, LANES), lambda p, i: (0, SB_KV_BLK0 + 2 * p + 1))],
        out_specs=pl.BlockSpec((SB_BLK, LANES), lambda p, i: (i, p)),
        out_shape=jax.ShapeDtypeStruct((t, N_SB_HEADS * HEAD_DIM), BF16),
        compiler_params=_cparams(("parallel", "arbitrary")),
        name="stick_breaking",
    )(main, main, main)


def _fox_kernel(cend_ref, q_ref, kve_ref, kvo_ref, c_ref, o_ref, kmax_ref):
    pair = pl.program_id(0)
    qi = pl.program_id(1)
    ratio = FOX_TQ // FOX_TK
    nkb = pl.num_programs(1) * ratio
    t = kve_ref.shape[0]
    scan = 1024

    @pl.when(qi == 0)
    def _():
        def max_norm(ref, keys_low):
            keep = (lax.broadcasted_iota(jnp.int32, (scan, LANES), 1) < HEAD_DIM) == keys_low

            def step(r, best):
                k = ref[pl.ds(pl.multiple_of(r * scan, scan), scan), :].astype(F32)
                k = jnp.where(keep, k, 0.0)
                return jnp.maximum(best, jnp.sum(k * k, axis=-1, keepdims=True))

            best = lax.fori_loop(0, t // scan, step, jnp.zeros((scan, 1), F32))
            return jnp.sqrt(jnp.max(best, axis=0, keepdims=True))

        kmax_ref[0:1, :] = jnp.broadcast_to(max_norm(kve_ref, True), (1, LANES))
        kmax_ref[1:2, :] = jnp.broadcast_to(max_norm(kvo_ref, False), (1, LANES))

    q_e, q_o, low = _pair_queries(q_ref)

    def logit_bound(q, row):
        qf = q.astype(F32)
        return jnp.sqrt(jnp.sum(qf * qf, axis=-1, keepdims=True)) * kmax_ref[row:row + 1, 0:1]

    bound = (logit_bound(q_e, 0), logit_bound(q_o, 1))
    rr = lax.broadcasted_iota(jnp.int32, (FOX_TQ, FOX_TK), 0)
    cc = lax.broadcasted_iota(jnp.int32, (FOX_TQ, FOX_TK), 1)

    key_lane = lax.broadcasted_iota(jnp.int32, (FOX_TK, LANES), 1)
    one_bf = jnp.ones((FOX_TK, LANES), BF16)

    def block_keys(kv_ref, kb):
        return kv_ref[pl.ds(pl.multiple_of(kb * FOX_TK, FOX_TK), FOX_TK), :]

    def scores(q, kv_ref, row, kb):
        k0 = pl.multiple_of(kb * FOX_TK, FOX_TK)
        return _dot_nt(q, block_keys(kv_ref, kb)) - c_ref[row:row + 1, pl.ds(k0, FOX_TK)]

    def consume(s, kv_ref, row, kb, mask, st):
        m, acc = st
        if mask is not None:
            s = jnp.where(mask, s, NEG_BIG)
        m_new = jnp.maximum(m, jnp.max(s, axis=-1, keepdims=True))
        p = jnp.exp(s - m_new)
        if mask is not None:
            p = jnp.where(mask, p, 0.0)
        vals = jnp.where(key_lane == row * HEAD_DIM, one_bf, block_keys(kv_ref, kb))
        return m_new, jnp.exp(m - m_new) * acc + _dot(p.astype(BF16), vals)

    def both_scores(kb):
        return scores(q_e, kve_ref, 0, kb), scores(q_o, kvo_ref, 1, kb)

    def both(s, kb, mask, st):
        return (consume(s[0], kve_ref, 0, kb, mask, st[0]), consume(s[1], kvo_ref, 1, kb, mask, st[1]))

    def needed(kb, st):
        kb = jnp.maximum(kb, 0)
        go = jnp.zeros((), jnp.bool_)
        for row in (0, 1):
            top = jnp.max(bound[row] - st[row][0])
            go = go | (top - cend_ref[(2 * pair + row) * nkb + kb] >= EXP_UNDERFLOW - BOUND_SLACK)
        return go.astype(jnp.int32)

    one = (jnp.full((FOX_TQ, 1), NEG_BIG, F32), jnp.zeros((FOX_TQ, LANES), F32))
    st = (one, one)
    for j in reversed(range(ratio)):
        kb = qi * ratio + j
        st = both(both_scores(kb), kb, rr - cc >= j * FOX_TK, st)

    def cond(c):
        return (c[0] >= 0) & (c[1] > 0)

    def body(c):
        st = both(both_scores(c[0]), c[0], None, c[2])
        return c[0] - 1, needed(c[0] - 1, st), st

    kb0 = qi * ratio - 1
    _, _, st = lax.while_loop(cond, body, (kb0, needed(kb0, st), st))
    (_, acc_e), (_, acc_o) = st
    o_e = acc_e / acc_e[:, 0:1]
    o_o = acc_o / acc_o[:, HEAD_DIM:HEAD_DIM + 1]
    o_ref[...] = jnp.where(low, o_o, o_e).astype(BF16)


def _forgetting(main, c, cend):
    t = main.shape[0]
    grid_spec = pltpu.PrefetchScalarGridSpec(
        num_scalar_prefetch=1,
        grid=(N_FOX_HEADS // 2, t // FOX_TQ),
        in_specs=[pl.BlockSpec((FOX_TQ, LANES), lambda p, i, ce: (i, FOX_Q_BLK0 + p)),
                  pl.BlockSpec((t, LANES), lambda p, i, ce: (0, FOX_KV_BLK0 + 2 * p)),
                  pl.BlockSpec((t, LANES), lambda p, i, ce: (0, FOX_KV_BLK0 + 2 * p + 1)),
                  pl.BlockSpec((None, 2, t), lambda p, i, ce: (p, 0, 0))],
        out_specs=pl.BlockSpec((FOX_TQ, LANES), lambda p, i, ce: (i, p)),
        scratch_shapes=[pltpu.VMEM((8, LANES), F32)],
    )
    return pl.pallas_call(
        _fox_kernel,
        grid_spec=grid_spec,
        out_shape=jax.ShapeDtypeStruct((t, N_FOX_HEADS * HEAD_DIM), BF16),
        compiler_params=_cparams(("parallel", "arbitrary")),
        name="forgetting_attention",
    )(cend, main, main, main, c)


def _layer_norm(x, g, b):
    mu = jnp.mean(x, axis=-1, keepdims=True)
    xc = x - mu
    var = jnp.mean(xc * xc, axis=-1, keepdims=True)
    return xc * lax.rsqrt(var + EPS) * g + b


def _outproj_kernel(yn_ref, ys_ref, yf_ref, h_ref, gn_ref, w_ref, g_ref, b_ref, o_ref):
    gn = gn_ref[...]
    parts = []
    lo = 0
    for y_ref in (yn_ref, ys_ref, yf_ref):
        yp = y_ref[...].astype(F32)
        hi = lo + yp.shape[1]
        ms = jnp.mean(yp * yp, axis=-1, keepdims=True)
        parts.append((yp * lax.rsqrt(ms + EPS) * gn[:, lo:hi]).astype(BF16))
        lo = hi
    mix = _dot(jnp.concatenate(parts, axis=-1), w_ref[...])
    o_ref[...] = _layer_norm(ALPHA * h_ref[...] + mix, g_ref[...], b_ref[...])


def _outproj(y_nsa, y_sb, y_fox, h, gn, w, g, b):
    t, d = h.shape
    tm = min(512, t)
    row = lambda i: (i, 0)
    full = lambda i: (0, 0)
    return pl.pallas_call(
        _outproj_kernel,
        grid=(t // tm,),
        in_specs=[pl.BlockSpec((tm, y_nsa.shape[1]), row), pl.BlockSpec((tm, y_sb.shape[1]), row),
                  pl.BlockSpec((tm, y_fox.shape[1]), row), pl.BlockSpec((tm, d), row), pl.BlockSpec((1, d), full),
                  pl.BlockSpec((d, d), full), pl.BlockSpec((1, d), full), pl.BlockSpec((1, d), full)],
        out_specs=pl.BlockSpec((tm, d), row),
        out_shape=jax.ShapeDtypeStruct((t, d), F32),
        compiler_params=_cparams(("parallel",)),
        name="norm_outproj_ln",
    )(y_nsa, y_sb, y_fox, h, gn, w, g, b)


def _moe_kernel(h_ref, rw_ref, rb_ref, wgu_ref, wd_ref, g_ref, b_ref, o_ref, xb_ref, gate_ref, acc_ref):
    e = pl.program_id(1)

    @pl.when(e == 0)
    def _():
        x = h_ref[...]
        xh, xm, _ = _split3(x)
        xb_ref[...] = xh
        w_hi = rw_ref[0]
        w_lo = rw_ref[1]
        logits = _dot(xh, w_hi) + _dot(xm, w_hi) + _dot(xh, w_lo) + rb_ref[...]
        lane = lax.broadcasted_iota(jnp.int32, logits.shape, 1)
        is_g = lane < N_GROUPS
        gl = jnp.where(is_g, logits, -jnp.inf)
        gmax = jnp.max(gl, axis=-1, keepdims=True)
        g_sel = jnp.min(jnp.where(gl == gmax, lane, LANES), axis=-1, keepdims=True)
        g_w = 1.0 / jnp.sum(jnp.where(is_g, jnp.exp(gl - gmax), 0.0), axis=-1, keepdims=True)
        ex = lane - N_GROUPS
        in_grp = (ex >= 0) & (ex < N_EXPERTS) & (ex // EXPERTS_PER_GROUP == g_sel)
        el = jnp.where(in_grp, logits, -jnp.inf)
        v1 = jnp.max(el, axis=-1, keepdims=True)
        i1 = jnp.min(jnp.where(el == v1, lane, LANES), axis=-1, keepdims=True)
        el2 = jnp.where(lane == i1, -jnp.inf, el)
        v2 = jnp.max(el2, axis=-1, keepdims=True)
        i2 = jnp.min(jnp.where(el2 == v2, lane, LANES), axis=-1, keepdims=True)
        e2 = jnp.exp(v2 - v1)
        w1 = g_w / (1.0 + e2)
        w2 = g_w * e2 / (1.0 + e2)
        gate_ref[...] = jnp.where(lane == i1, w1, 0.0) + jnp.where(lane == i2, w2, 0.0)
        acc_ref[...] = jnp.zeros_like(acc_ref)

    xb = xb_ref[...]
    lane = lax.broadcasted_iota(jnp.int32, gate_ref.shape, 1)
    update = jnp.zeros(acc_ref.shape, F32)
    for j in range(MOE_EXPERTS_PER_STEP):
        gu = _dot(xb, wgu_ref[j])
        expert_lane = e * MOE_EXPERTS_PER_STEP + j + N_GROUPS
        gcol = jnp.sum(jnp.where(lane == expert_lane, gate_ref[...], 0.0), axis=-1, keepdims=True)
        hid = jax.nn.silu(gu[:, :D_FF]) * gu[:, D_FF:]
        update = update + _dot((hid * gcol).astype(BF16), wd_ref[j])
    acc_ref[...] += update

    @pl.when(e == N_EXPERTS // MOE_EXPERTS_PER_STEP - 1)
    def _():
        o_ref[...] = _layer_norm(ALPHA * h_ref[...] + acc_ref[...], g_ref[...], b_ref[...])


def _moe(h, rw, rb, wgu, wd, g, b):
    t, d = h.shape
    tm = min(512, t)
    return pl.pallas_call(
        _moe_kernel,
        grid=(t // tm, N_EXPERTS // MOE_EXPERTS_PER_STEP),
        in_specs=[pl.BlockSpec((tm, d), lambda i, e: (i, 0)),
                  pl.BlockSpec((2, d, LANES), lambda i, e: (0, 0, 0)),
                  pl.BlockSpec((1, LANES), lambda i, e: (0, 0)),
                  pl.BlockSpec((MOE_EXPERTS_PER_STEP, d, 2 * D_FF), lambda i, e: (e, 0, 0)),
                  pl.BlockSpec((MOE_EXPERTS_PER_STEP, D_FF, d), lambda i, e: (e, 0, 0)),
                  pl.BlockSpec((1, d), lambda i, e: (0, 0)),
                  pl.BlockSpec((1, d), lambda i, e: (0, 0))],
        out_specs=pl.BlockSpec((tm, d), lambda i, e: (i, 0)),
        out_shape=jax.ShapeDtypeStruct((t, d), F32),
        scratch_shapes=[pltpu.VMEM((tm, d), BF16), pltpu.VMEM((tm, LANES), F32), pltpu.VMEM((tm, d), F32)],
        compiler_params=_cparams(("parallel", "arbitrary")),
        name="moe_ln",
    )(h, rw, rb, wgu, wd, g, b)


def _cmp_to_slc_matrix(n_cmp_pad, n_slc):
    r, c = SLC_LEN // CMP_STRIDE, CMP_LEN // CMP_STRIDE
    m = np.zeros((n_cmp_pad, n_slc), np.float32)
    for j in range(n_slc):
        for a in range(r):
            for bb in range(c):
                i = r * j + a + bb
                if i < n_cmp_pad - 1:
                    m[i, j] += 1.0
    return m


def _compress_weights(pos, w1, w2):
    eye = jnp.eye(N_NSA_KV, dtype=F32)
    half = CMP_LEN // 2
    cols = N_NSA_KV * HEAD_DIM

    def expand(w):
        return jnp.einsum('lde,gh->lgdhe', w, eye).reshape(half * cols, cols).astype(BF16)

    def tile_pos(p):
        return jnp.broadcast_to(p[:, None, :], (half, N_NSA_KV, HEAD_DIM)).reshape(1, half * cols)

    pos2 = jnp.concatenate([tile_pos(pos[:half]), tile_pos(pos[half:])], axis=0)
    w2bd = jnp.einsum('ef,gh->gehf', w2, eye).reshape(cols, cols).astype(BF16)
    return pos2, expand(w1[:half]), expand(w1[half:]), w2bd


def _mixer(h, w_in, cmp_pos_k, cmp_w1_k, cmp_w2_k, cmp_pos_v, cmp_w1_v, cmp_w2_v, fox_forget_bias,
           norm_nsa, norm_sb, norm_fox, w_out, ln_g, ln_b):
    t = h.shape[0]
    nq = t // NSA_QB
    w_main = w_in[:, _PERM_MAIN].astype(BF16)
    w_tail = jnp.pad(w_in[:, _PERM_TAIL], ((0, 0), (0, LANES - _PERM_TAIL.size))).astype(BF16)
    main, tail = _inproj(h, w_main, w_tail)

    n_rows = t // CMP_STRIDE
    kc = _compress(main[:, 1024:1280].reshape(n_rows, -1), *_compress_weights(cmp_pos_k, cmp_w1_k, cmp_w2_k))
    vc = _compress(main[:, 1280:1536].reshape(n_rows, -1), *_compress_weights(cmp_pos_v, cmp_w1_v, cmp_w2_v))
    kvc = jnp.concatenate([kc.reshape(n_rows, N_NSA_KV, HEAD_DIM), vc.reshape(n_rows, N_NSA_KV, HEAD_DIM)],
                          axis=-1).transpose(1, 0, 2)
    slopes = 2.0 **(-8.0 * jnp.arange(1, N_NSA_HEADS + 1, dtype=F32) / N_NSA_HEADS)
    slopes = jnp.repeat(slopes.reshape(N_NSA_KV, NSA_HPG), NSA_QB, axis=1)[..., None]
    imap = jnp.asarray(_cmp_to_slc_matrix(n_rows, t // SLC_LEN), BF16)
    oc, sel, flags = _nsa_cmp(main, kvc, imap, slopes)
    n_chunks = t // SLC_CHUNK
    diag = (jnp.arange(nq) * NSA_QB) // SLC_CHUNK
    active = flags[:, :, 0, :n_chunks] * (jnp.arange(n_chunks)[None, None, :] != diag[None, :, None])
    idx = jnp.argsort(1 - active, axis=-1, stable=True).astype(jnp.int32).reshape(-1)
    cnt = jnp.sum(active, axis=-1).astype(jnp.int32).reshape(-1)
    gates = tail[:, :48].reshape(nq, NSA_QB, N_NSA_KV, NSA_HPG, 3).transpose(2, 0, 3, 1, 4)
    gates = gates.reshape(N_NSA_KV, nq, NSA_HPG * NSA_QB, 3)
    o_nsa = _nsa_main(idx, cnt, main, sel, oc, gates, slopes)

    o_sb = _stick_breaking(main)

    nb = t // LANES
    ff_rows = tail[:, 48:56].T.reshape(N_FOX_HEADS * nb, LANES)
    bias_rows = jnp.repeat(fox_forget_bias, nb)[:, None]
    c = _forget_cumsum(ff_rows, bias_rows, nb).reshape(N_FOX_HEADS, t)
    cend = c[:, FOX_TK - 1::FOX_TK].reshape(-1)
    o_fox = _forgetting(main, c.reshape(N_FOX_HEADS // 2, 2, t), cend)

    gn = jnp.concatenate([norm_nsa, norm_sb, norm_fox])[_PERM_MIX][None, :]
    return _outproj(o_nsa, o_sb, o_fox, h, gn, w_out[_PERM_MIX].astype(BF16), ln_g[None, :], ln_b[None, :])


def _ffn(h, rg_w, rg_b, re_w, re_b, w_gate, w_up, w_down, ln_g, ln_b):
    rw = jnp.pad(jnp.concatenate([rg_w, re_w], axis=1), ((0, 0), (0, LANES - N_GROUPS - N_EXPERTS)))
    rw_hi = rw.astype(BF16)
    rw_lo = (rw - rw_hi.astype(F32)).astype(BF16)
    rb = jnp.pad(jnp.concatenate([rg_b, re_b]), (0, LANES - N_GROUPS - N_EXPERTS))[None, :]
    wgu = jnp.concatenate([w_gate, w_up], axis=-1).astype(BF16)
    return _moe(h, jnp.stack([rw_hi, rw_lo]), rb, wgu, w_down.astype(BF16), ln_g[None, :], ln_b[None, :])


def kernel(x, w_in, cmp_pos_k, cmp_w1_k, cmp_w2_k, cmp_pos_v, cmp_w1_v, cmp_w2_v, fox_forget_bias, norm_nsa, norm_sb, norm_fox, w_out, ln1_g, ln1_b, router_group_w, router_group_b, router_expert_w, router_expert_b, expert_w_gate, expert_w_up, expert_w_down, ln2_g, ln2_b):
    b, t, d = x.shape
    assert b == 1 and d == D_MODEL and t % 1024 == 0
    h = x.reshape(t, d)
    for l in range(DEPTH):
        h = _mixer(h, w_in[l], cmp_pos_k[l], cmp_w1_k[l], cmp_w2_k[l], cmp_pos_v[l], cmp_w1_v[l], cmp_w2_v[l],
                   fox_forget_bias[l], norm_nsa[l], norm_sb[l], norm_fox[l], w_out[l], ln1_g[l], ln1_b[l])
        h = _ffn(h, router_group_w[l], router_group_b[l], router_expert_w[l], router_expert_b[l],
                 expert_w_gate[l], expert_w_up[l], expert_w_down[l], ln2_g[l], ln2_b[l])
    return h.reshape(b, t, d)
```
